```python
import jax, jax.numpy as jnp
from jax import lax
import numpy as np

D_MODEL = 1024
BATCH = 8
SEQ = 2048
DEPTH = 1
DEC_BATCH = 128
DEC_SEQ = 8
PAST_LEN = 16384
PAGE_SIZE = 128

N_META = 16
MIX_WIDTH = D_MODEL
A_HEADS = 8
A_HEAD_DIM = 64
A_WIDTH = A_HEADS * A_HEAD_DIM
B_HEADS = 8
B_HEAD_DIM = 64
B_WIDTH = B_HEADS * B_HEAD_DIM
IN_COLS = 2 * A_WIDTH + 3 * B_WIDTH
CONV_A_W = 31
CONV_B_W = 3
CONV_F_W = 3
D_FF = 2816
EPS = 1e-6

kernel_name = "hymba_conformer_shortconv_convffn_step"


def _rmsnorm(x, g):
    xf = x.astype(jnp.float32)
    r = lax.rsqrt(jnp.mean(xf * xf, axis=-1, keepdims=True) + EPS)
    return (xf * r).astype(x.dtype) * g


def _head_layernorm(u, g, b):
    n, t, c = u.shape
    uf = u.astype(jnp.float32).reshape(n, t, A_HEADS, A_HEAD_DIM)
    mu = jnp.mean(uf, axis=-1, keepdims=True)
    var = jnp.mean(jnp.square(uf - mu), axis=-1, keepdims=True)
    un = ((uf - mu) * lax.rsqrt(var + EPS)).reshape(n, t, c).astype(u.dtype)
    return un * g + b


def _causal_dwconv(x, buf, w):
    xe = jnp.concatenate([buf.astype(x.dtype), x], axis=1)
    c = x.shape[-1]
    y = lax.conv_general_dilated(
        xe, w[:, None, :].astype(x.dtype), window_strides=(1,), padding='VALID',
        dimension_numbers=('NWC', 'WIO', 'NWC'), feature_group_count=c)
    return y, xe[:, -(w.shape[0] - 1):, :]


def _layer(x, buf_a, buf_b, buf_f, norm_mix_g, w_in, w_conv_a, b_conv_a, gn_a_g, gn_a_b,
           w_conv_b, beta_a, beta_b, w_out, norm_ffn_g, w_up, w_conv_f, w_down):
    h = _rmsnorm(x, norm_mix_g)
    proj = jnp.einsum('ntd,dc->ntc', h, w_in)
    a_val, a_gate, b_gate, c_gate, b_in = jnp.split(
        proj, [A_WIDTH, 2 * A_WIDTH, 2 * A_WIDTH + B_WIDTH, 2 * A_WIDTH + 2 * B_WIDTH], axis=-1)
    u = a_val * jax.nn.sigmoid(a_gate)
    ua, nbuf_a = _causal_dwconv(u, buf_a, w_conv_a)
    ya = jax.nn.silu(_head_layernorm(ua + b_conv_a, gn_a_g, gn_a_b)) * beta_a
    z = c_gate * b_in
    zb, nbuf_b = _causal_dwconv(z, buf_b, w_conv_b)
    yb = b_gate * zb * beta_b
    x = x + jnp.einsum('ntc,cd->ntd', jnp.concatenate([ya, yb], axis=-1), w_out)
    h = _rmsnorm(x, norm_ffn_g)
    up = jnp.einsum('ntd,df->ntf', h, w_up)
    upc, nbuf_f = _causal_dwconv(up, buf_f, w_conv_f)
    gate, val = jnp.split(upc, 2, axis=-1)
    x = x + jnp.einsum('ntf,fd->ntd', jax.nn.silu(gate) * val, w_down)
    return x, nbuf_a, nbuf_b, nbuf_f


def setup_inputs(seed: int = 0) -> dict:
    key = jax.random.key(seed)
    ks = jax.random.split(key, 24)
    f32 = jnp.float32
    nrm = lambda k, s, sc: jax.random.normal(k, s, f32) * sc
    return {
        "x_prompt": nrm(ks[0], (BATCH, SEQ, D_MODEL), 1.0),
        "x_sample": nrm(ks[1], (DEC_BATCH, DEC_SEQ, D_MODEL), 1.0),
        "state_conv_a": nrm(ks[2], (DEPTH, DEC_BATCH, CONV_A_W - 1, A_WIDTH), 0.5),
        "state_conv_b": nrm(ks[3], (DEPTH, DEC_BATCH, CONV_B_W - 1, B_WIDTH), 0.5),
        "state_conv_ffn": nrm(ks[4], (DEPTH, DEC_BATCH, CONV_F_W - 1, 2 * D_FF), 1.0),
        "meta_tokens": nrm(ks[5], (N_META, D_MODEL), 1.0),
        "norm_mix_g": 1.0 + nrm(ks[6], (DEPTH, D_MODEL), 0.02),
        "w_in": nrm(ks[7], (DEPTH, D_MODEL, IN_COLS), D_MODEL ** -0.5),
        "w_conv_a": nrm(ks[8], (DEPTH, CONV_A_W, A_WIDTH), CONV_A_W ** -0.5),
        "b_conv_a": nrm(ks[9], (DEPTH, A_WIDTH), 0.02),
        "gn_a_g": 1.0 + nrm(ks[10], (DEPTH, A_WIDTH), 0.02),
        "gn_a_b": nrm(ks[11], (DEPTH, A_WIDTH), 0.02),
        "w_conv_b": nrm(ks[12], (DEPTH, CONV_B_W, B_WIDTH), CONV_B_W ** -0.5),
        "beta_a": 1.0 + nrm(ks[13], (DEPTH, A_WIDTH), 0.02),
        "beta_b": 1.0 + nrm(ks[14], (DEPTH, B_WIDTH), 0.02),
        "w_out": nrm(ks[15], (DEPTH, MIX_WIDTH, D_MODEL), MIX_WIDTH ** -0.5),
        "norm_ffn_g": 1.0 + nrm(ks[16], (DEPTH, D_MODEL), 0.02),
        "w_up": nrm(ks[17], (DEPTH, D_MODEL, 2 * D_FF), D_MODEL ** -0.5),
        "w_conv_f": nrm(ks[18], (DEPTH, CONV_F_W, 2 * D_FF), CONV_F_W ** -0.5),
        "w_down": nrm(ks[19], (DEPTH, D_FF, D_MODEL), D_FF ** -0.5),
        "norm_final_g": 1.0 + nrm(ks[20], (D_MODEL,), 0.02),
    }


def reference(x_prompt, x_sample, state_conv_a, state_conv_b, state_conv_ffn, meta_tokens,
              norm_mix_g, w_in, w_conv_a, b_conv_a, gn_a_g, gn_a_b, w_conv_b, beta_a, beta_b,
              w_out, norm_ffn_g, w_up, w_conv_f, w_down, norm_final_g):
    dt = x_prompt.dtype
    meta = jnp.broadcast_to(meta_tokens.astype(dt)[None], (BATCH, N_META, D_MODEL))
    xp = jnp.concatenate([meta, x_prompt], axis=1)
    xs = x_sample
    pa, pb, pf, sa, sb, sf = [], [], [], [], [], []
    for l in range(DEPTH):
        lw = (norm_mix_g[l], w_in[l], w_conv_a[l], b_conv_a[l], gn_a_g[l], gn_a_b[l],
              w_conv_b[l], beta_a[l], beta_b[l], w_out[l], norm_ffn_g[l], w_up[l],
              w_conv_f[l], w_down[l])
        zero_a = jnp.zeros((BATCH, CONV_A_W - 1, A_WIDTH), dt)
        zero_b = jnp.zeros((BATCH, CONV_B_W - 1, B_WIDTH), dt)
        zero_f = jnp.zeros((BATCH, CONV_F_W - 1, 2 * D_FF), dt)
        xp, na, nb, nf = _layer(xp, zero_a, zero_b, zero_f, *lw)
        pa.append(na); pb.append(nb); pf.append(nf)
        xs, na, nb, nf = _layer(xs, state_conv_a[l], state_conv_b[l], state_conv_ffn[l], *lw)
        sa.append(na); sb.append(nb); sf.append(nf)
    y_prompt = _rmsnorm(xp[:, N_META:, :], norm_final_g)
    y_sample = _rmsnorm(xs, norm_final_g)
    return (y_prompt, y_sample, jnp.stack(pa), jnp.stack(pb), jnp.stack(pf),
            jnp.stack(sa), jnp.stack(sb), jnp.stack(sf))
```

```python
import functools

import jax
import jax.numpy as jnp
from jax import lax
from jax.experimental import pallas as pl
from jax.experimental.pallas import tpu as pltpu

EPS = 1e-6
HEAD_DIM = 64
LANES = 128
SUBLANES = 8
FF_CHUNK = 256
VMEM_LIMIT_BYTES = 56 * 1024 * 1024

_F32 = jnp.float32
_BF16 = jnp.bfloat16


def _rmsnorm(x, g):
    ms = jnp.mean(x * x, axis=-1, keepdims=True)
    return x * lax.rsqrt(ms + EPS) * g


def _sigmoid(x):
    return 1.0 / (1.0 + jnp.exp(-x))


def _silu(x):
    return x * _sigmoid(x)


def _group_mean(v, lane_lo):
    sa = jnp.sum(jnp.where(lane_lo, v, 0.0), axis=1, keepdims=True)
    sb = jnp.sum(jnp.where(lane_lo, 0.0, v), axis=1, keepdims=True)
    return jnp.where(lane_lo, sa, sb) * (1.0 / HEAD_DIM)


def _head_layernorm(ua):
    rows, width = ua.shape
    lane_lo = lax.broadcasted_iota(jnp.int32, (rows, LANES), 1) < HEAD_DIM
    outs = []
    for j in range(width // LANES):
        xb = ua[:, j * LANES:(j + 1) * LANES]
        d = xb - _group_mean(xb, lane_lo)
        var = _group_mean(d * d, lane_lo)
        outs.append(d * lax.rsqrt(var + EPS))
    return jnp.concatenate(outs, axis=1)


def _group_a_out(ua, lng, lnb, beta_a):
    return _silu(_head_layernorm(ua) * lng + lnb) * beta_a


def _slab_store(buf, row0, val):
    for c in range(buf.shape[0]):
        buf[c, pl.ds(row0, val.shape[0]), :] = val[:, c * LANES:(c + 1) * LANES]


def _slab_load(buf, row0, nrows):
    return jnp.concatenate(
        [buf[c, pl.ds(row0, nrows), :] for c in range(buf.shape[0])], axis=1)


def _slab_window(buf, row0, nrows):
    return jnp.concatenate(
        [buf[c, pl.ds(row0, nrows, stride=1), :] for c in range(buf.shape[0])], axis=1)


def _seq_kernel(x_ref, ca0_ref, cb0_ref, cf0_ref, g1_ref, win_ref, wa_ref, ba_ref,
                lng_ref, lnb_ref, wb_ref, beta_a_ref, beta_b_ref, wout_ref, g2_ref,
                wup_ref, wcf_ref, wdown_ref, gf_ref,
                y_ref, ca_ref, cb_ref, cf_ref,
                ubuf, zbuf, ucar, upbuf, h_ref, proj_ref, mix_ref, x1_ref, act_ref,
                *, tile, rows, a_width, b_width, conv_a_w, n_chunks):
    t = pl.program_id(1)
    nt = pl.num_programs(1)
    ha = ubuf.shape[1] - tile
    hb = zbuf.shape[1] - tile
    off_a = ha - (conv_a_w - 1)
    n_row_chunks = tile // rows

    @pl.when(t == 0)
    def _():
        _slab_store(ubuf, 0, ca0_ref[...])
        _slab_store(zbuf, 0, cb0_ref[...])
        ucar[...] = cf0_ref[...]

    def norm1(i, c):
        r0 = pl.multiple_of(i * rows, rows)
        x = x_ref[pl.ds(r0, rows), :]
        h_ref[pl.ds(r0, rows), :] = _rmsnorm(x, g1_ref[...]).astype(_BF16)
        return c
    lax.fori_loop(0, n_row_chunks, norm1, 0)

    proj_ref[...] = jnp.dot(h_ref[...], win_ref[...], preferred_element_type=_F32)

    def mixers(i, c):
        r0 = pl.multiple_of(i * rows, rows)
        p = lambda lo, w: proj_ref[pl.ds(r0, rows), lo:lo + w]
        u = p(0, a_width) * _sigmoid(p(a_width, a_width))
        _slab_store(ubuf, ha + r0, u)
        acc = jnp.broadcast_to(ba_ref[...], (rows, a_width))
        for k in range(conv_a_w):
            acc = acc + _slab_window(ubuf, r0 + off_a + k, rows) * wa_ref[k:k + 1, :]
        ya = _group_a_out(acc, lng_ref[...], lnb_ref[...], beta_a_ref[...])
        mix_ref[pl.ds(r0, rows), 0:a_width] = ya.astype(_BF16)

        o = 2 * a_width
        z = p(o + b_width, b_width) * p(o + 2 * b_width, b_width)
        _slab_store(zbuf, hb + r0, z)
        zb = (_slab_window(zbuf, r0 + hb - 2, rows) * wb_ref[0:1, :]
              + _slab_window(zbuf, r0 + hb - 1, rows) * wb_ref[1:2, :]
              + z * wb_ref[2:3, :])
        yb = p(o, b_width) * zb * beta_b_ref[...]
        mix_ref[pl.ds(r0, rows), a_width:a_width + b_width] = yb.astype(_BF16)
        return c
    lax.fori_loop(0, n_row_chunks, mixers, 0)

    x1_ref[...] = x_ref[...] + jnp.dot(mix_ref[...], wout_ref[...],
                                       preferred_element_type=_F32)

    def norm2(i, c):
        r0 = pl.multiple_of(i * rows, rows)
        x = x1_ref[pl.ds(r0, rows), :]
        h_ref[pl.ds(r0, rows), :] = _rmsnorm(x, g2_ref[...]).astype(_BF16)
        return c
    lax.fori_loop(0, n_row_chunks, norm2, 0)

    def ffn(j, c):
        _slab_store(upbuf, 0, ucar[j])
        _slab_store(upbuf, hb, jnp.dot(h_ref[...], wup_ref[j],
                                       preferred_element_type=_F32))
        ucar[j] = _slab_load(upbuf, tile, hb)
        w = wcf_ref[j]

        def ffn_rows(i, c2):
            r0 = pl.multiple_of(i * rows, rows)
            cv = (_slab_window(upbuf, r0 + hb - 2, rows) * w[0:1, :]
                  + _slab_window(upbuf, r0 + hb - 1, rows) * w[1:2, :]
                  + _slab_load(upbuf, r0 + hb, rows) * w[2:3, :])
            a = _silu(cv[:, 0:FF_CHUNK]) * cv[:, FF_CHUNK:2 * FF_CHUNK]
            act_ref[j, pl.ds(r0, rows), :] = a.astype(_BF16)
            return c2
        lax.fori_loop(0, n_row_chunks, ffn_rows, 0)
        return c
    lax.fori_loop(0, n_chunks, ffn, 0)

    acc = x1_ref[...]
    for j in range(n_chunks):
        acc = acc + jnp.dot(act_ref[j], wdown_ref[j], preferred_element_type=_F32)
    x1_ref[...] = acc

    def norm3(i, c):
        r0 = pl.multiple_of(i * rows, rows)
        x = x1_ref[pl.ds(r0, rows), :]
        y_ref[pl.ds(r0, rows), :] = _rmsnorm(x, gf_ref[...])
        return c
    lax.fori_loop(0, n_row_chunks, norm3, 0)

    _slab_store(ubuf, 0, _slab_load(ubuf, tile, ha))
    _slab_store(zbuf, 0, _slab_load(zbuf, tile, hb))

    @pl.when(t == nt - 1)
    def _():
        ca_ref[...] = _slab_load(ubuf, 0, ha)
        cb_ref[...] = _slab_load(zbuf, 0, hb)
        cf_ref[...] = ucar[...]


def _const_spec(shape):
    nd = len(shape)
    return pl.BlockSpec(shape, lambda b, t: (0,) * nd, pipeline_mode=pl.Buffered(1))


def _seq_layer(x, ca0, cb0, cf0, wts, *, tile, rows, name):
    nb, length, d = x.shape
    nt = length // tile
    (g1, win, wa, ba, lng, lnb, wb, beta_a, beta_b, wout, g2, wup, wcf, wdown, gf) = wts
    a_width = wa.shape[1]
    b_width = wb.shape[1]
    conv_a_w = wa.shape[0]
    n_chunks = wup.shape[0]
    ha, hb = ca0.shape[0], cb0.shape[0]
    kern = functools.partial(_seq_kernel, tile=tile, rows=rows, a_width=a_width,
                             b_width=b_width, conv_a_w=conv_a_w, n_chunks=n_chunks)
    consts = (ca0, cb0, cf0) + tuple(wts)
    in_specs = [pl.BlockSpec((None, tile, d), lambda b, t: (b, t, 0))]
    in_specs += [_const_spec(c.shape) for c in consts]
    out_shape = (
        jax.ShapeDtypeStruct((nb, length, d), _F32),
        jax.ShapeDtypeStruct((nb, ha, a_width), _F32),
        jax.ShapeDtypeStruct((nb, hb, b_width), _F32),
        jax.ShapeDtypeStruct((nb,) + cf0.shape, _F32),
    )
    out_specs = (
        pl.BlockSpec((None, tile, d), lambda b, t: (b, t, 0)),
        pl.BlockSpec((None, ha, a_width), lambda b, t: (b, 0, 0)),
        pl.BlockSpec((None, hb, b_width), lambda b, t: (b, 0, 0)),
        pl.BlockSpec((None,) + cf0.shape, lambda b, t: (b, 0, 0, 0)),
    )
    scratch = [
        pltpu.VMEM((a_width // LANES, ha + tile, LANES), _F32),
        pltpu.VMEM((b_width // LANES, hb + tile, LANES), _F32),
        pltpu.VMEM(cf0.shape, _F32),
        pltpu.VMEM((2 * FF_CHUNK // LANES, hb + tile, LANES), _F32),
        pltpu.VMEM((tile, d), _BF16),
        pltpu.VMEM((tile, win.shape[1]), _F32),
        pltpu.VMEM((tile, a_width + b_width), _BF16),
        pltpu.VMEM((tile, d), _F32),
        pltpu.VMEM((n_chunks, tile, FF_CHUNK), _BF16),
    ]
    return pl.pallas_call(
        kern,
        grid=(nb, nt),
        in_specs=in_specs,
        out_specs=out_specs,
        out_shape=out_shape,
        scratch_shapes=scratch,
        compiler_params=pltpu.CompilerParams(
            dimension_semantics=("arbitrary", "arbitrary"),
            vmem_limit_bytes=VMEM_LIMIT_BYTES),
        name=name,
    )(x, *consts)


def _sample_kernel(xs_ref, sa_ref, sb_ref, sf_ref, g1_ref, win_ref, wa_ref, ba_ref,
                   lng_ref, lnb_ref, wb_ref, beta_a_ref, beta_b_ref, wout_ref, g2_ref,
                   wup_ref, wcf_ref, wdown_ref, gf_ref,
                   ys_ref, na_ref, nb_ref, nf_ref,
                   h_ref, proj_ref, u_ref, z_ref, mix_ref, x1_ref, up_ref, act_ref,
                   *, steps, d, a_width, b_width, conv_a_w, d_ff, n_chunks):
    s = xs_ref.shape[0]
    hist_a = conv_a_w - 1
    rows_of = lambda t: slice(t * s, (t + 1) * s)
    cols = lambda t, w: slice(t * w, (t + 1) * w)

    for t in range(steps):
        h_ref[rows_of(t), :] = _rmsnorm(xs_ref[:, cols(t, d)], g1_ref[...]).astype(_BF16)
    proj_ref[...] = jnp.dot(h_ref[...], win_ref[...], preferred_element_type=_F32)

    o = 2 * a_width
    for t in range(steps):
        r = rows_of(t)
        u_ref[r, :] = proj_ref[r, 0:a_width] * _sigmoid(proj_ref[r, a_width:o])
        z_ref[r, :] = (proj_ref[r, o + b_width:o + 2 * b_width]
                       * proj_ref[r, o + 2 * b_width:o + 3 * b_width])

    def xe_a(j):
        return sa_ref[:, cols(j, a_width)] if j < hist_a else u_ref[rows_of(j - hist_a), :]

    def xe_b(j):
        return sb_ref[:, cols(j, b_width)] if j < 2 else z_ref[rows_of(j - 2), :]

    for t in range(steps):
        r = rows_of(t)
        acc = jnp.broadcast_to(ba_ref[...], (s, a_width))
        for k in range(conv_a_w):
            acc = acc + xe_a(t + k) * wa_ref[k:k + 1, :]
        ya = _group_a_out(acc, lng_ref[...], lnb_ref[...], beta_a_ref[...])
        mix_ref[r, 0:a_width] = ya.astype(_BF16)
        zb = (xe_b(t) * wb_ref[0:1, :] + xe_b(t + 1) * wb_ref[1:2, :]
              + xe_b(t + 2) * wb_ref[2:3, :])
        yb = proj_ref[r, o:o + b_width] * zb * beta_b_ref[...]
        mix_ref[r, a_width:a_width + b_width] = yb.astype(_BF16)

    for j in range(hist_a):
        na_ref[:, cols(j, a_width)] = xe_a(j + steps)
    for j in range(2):
        nb_ref[:, cols(j, b_width)] = xe_b(j + steps)

    x1_ref[...] = jnp.dot(mix_ref[...], wout_ref[...], preferred_element_type=_F32)
    for t in range(steps):
        r = rows_of(t)
        x1 = x1_ref[r, :] + xs_ref[:, cols(t, d)]
        x1_ref[r, :] = x1
        h_ref[r, :] = _rmsnorm(x1, g2_ref[...]).astype(_BF16)

    for j in range(n_chunks):
        up_ref[...] = jnp.dot(h_ref[...], wup_ref[j], preferred_element_type=_F32)
        w = wcf_ref[j]
        lo = j * FF_CHUNK

        def xe_f(i, half):
            if i < 2:
                c0 = i * 2 * d_ff + half * d_ff + lo
                return sf_ref[:, c0:c0 + FF_CHUNK]
            return up_ref[rows_of(i - 2), half * FF_CHUNK:(half + 1) * FF_CHUNK]

        for t in range(steps):
            gv = []
            for half in range(2):
                wh = w[:, half * FF_CHUNK:(half + 1) * FF_CHUNK]
                gv.append(xe_f(t, half) * wh[0:1, :] + xe_f(t + 1, half) * wh[1:2, :]
                          + xe_f(t + 2, half) * wh[2:3, :])
            act_ref[j, rows_of(t), :] = (_silu(gv[0]) * gv[1]).astype(_BF16)
        for i in range(2):
            for half in range(2):
                c0 = i * 2 * d_ff + half * d_ff + lo
                nf_ref[:, c0:c0 + FF_CHUNK] = xe_f(i + steps, half)

    acc = x1_ref[...]
    for j in range(n_chunks):
        acc = acc + jnp.dot(act_ref[j], wdown_ref[j], preferred_element_type=_F32)
    for t in range(steps):
        ys_ref[:, cols(t, d)] = _rmsnorm(acc[rows_of(t), :], gf_ref[...])


def _sample_layer(xs, sa, sb, sf, wts, *, steps, seq_block):
    n, _ = xs.shape
    (g1, win, wa, ba, lng, lnb, wb, beta_a, beta_b, wout, g2, wup, wcf, wdown, gf) = wts
    d = g1.shape[1]
    a_width = wa.shape[1]
    b_width = wb.shape[1]
    conv_a_w = wa.shape[0]
    n_chunks = wup.shape[0]
    d_ff = n_chunks * FF_CHUNK
    m = steps * seq_block
    kern = functools.partial(_sample_kernel, steps=steps, d=d, a_width=a_width,
                             b_width=b_width, conv_a_w=conv_a_w, d_ff=d_ff,
                             n_chunks=n_chunks)
    row_spec = lambda a: pl.BlockSpec((seq_block, a.shape[1]), lambda i: (i, 0))
    const_spec = lambda a: pl.BlockSpec(a.shape, lambda i: (0,) * a.ndim,
                                        pipeline_mode=pl.Buffered(1))
    data = (xs, sa, sb, sf)
    in_specs = [row_spec(a) for a in data] + [const_spec(a) for a in wts]
    out_shape = tuple(jax.ShapeDtypeStruct(a.shape, _F32) for a in data)
    out_specs = tuple(row_spec(a) for a in data)
    scratch = [
        pltpu.VMEM((m, d), _BF16),
        pltpu.VMEM((m, win.shape[1]), _F32),
        pltpu.VMEM((m, a_width), _F32),
        pltpu.VMEM((m, b_width), _F32),
        pltpu.VMEM((m, a_width + b_width), _BF16),
        pltpu.VMEM((m, d), _F32),
        pltpu.VMEM((m, 2 * FF_CHUNK), _F32),
        pltpu.VMEM((n_chunks, m, FF_CHUNK), _BF16),
    ]
    return pl.pallas_call(
        kern,
        grid=(n // seq_block,),
        in_specs=in_specs,
        out_specs=out_specs,
        out_shape=out_shape,
        scratch_shapes=scratch,
        compiler_params=pltpu.CompilerParams(
            dimension_semantics=("arbitrary",),
            vmem_limit_bytes=VMEM_LIMIT_BYTES),
        name="sample_layer",
    )(*data, *wts)


def _ff_chunked_cols(a, d_ff):
    n_chunks = d_ff // FF_CHUNK
    lead = a.shape[:-1]
    a = a.reshape(lead + (2, n_chunks, FF_CHUNK))
    a = jnp.moveaxis(a, -2, 0)
    return a.reshape((n_chunks,) + lead + (2 * FF_CHUNK,))


def _ff_unchunked_cols(a):
    n_chunks, rows = a.shape[-3], a.shape[-2]
    lead = a.shape[:-3]
    a = a.reshape(lead + (n_chunks, rows, 2, FF_CHUNK))
    a = jnp.moveaxis(a, -4, -2)
    return a.reshape(lead + (rows, 2 * n_chunks * FF_CHUNK))


def kernel(x_prompt, x_sample, state_conv_a, state_conv_b, state_conv_ffn, meta_tokens,
           norm_mix_g, w_in, w_conv_a, b_conv_a, gn_a_g, gn_a_b, w_conv_b, beta_a, beta_b,
           w_out, norm_ffn_g, w_up, w_conv_f, w_down, norm_final_g):
    depth = w_in.shape[0]
    assert depth == 1, "single-layer step only"
    batch, seq, d = x_prompt.shape
    n_dec, dec_seq, _ = x_sample.shape
    n_meta = meta_tokens.shape[0]
    conv_a_w, a_width = w_conv_a.shape[1:]
    b_width = w_conv_b.shape[2]
    d_ff = w_down.shape[1]
    assert a_width % LANES == 0 and LANES == 2 * HEAD_DIM
    assert d_ff % FF_CHUNK == 0 and w_conv_b.shape[1] == 3 and w_conv_f.shape[1] == 3
    n_chunks = d_ff // FF_CHUNK
    hist_a = conv_a_w - 1
    ha = -(-hist_a // SUBLANES) * SUBLANES
    hb = SUBLANES

    row = lambda v: v.reshape(1, -1)
    wts = (
        row(norm_mix_g[0]),
        w_in[0].astype(_BF16),
        w_conv_a[0], row(b_conv_a[0]), row(gn_a_g[0]), row(gn_a_b[0]),
        w_conv_b[0], row(beta_a[0]), row(beta_b[0]),
        w_out[0].astype(_BF16),
        row(norm_ffn_g[0]),
        _ff_chunked_cols(w_up[0].astype(_BF16), d_ff),
        _ff_chunked_cols(w_conv_f[0], d_ff),
        w_down[0].astype(_BF16).reshape(n_chunks, FF_CHUNK, d),
        row(norm_final_g),
    )

    zeros = lambda *s: jnp.zeros(s, _F32)
    _, ca_m, cb_m, cf_m = _seq_layer(
        meta_tokens[None], zeros(ha, a_width), zeros(hb, b_width),
        zeros(n_chunks, hb, 2 * FF_CHUNK), wts, tile=n_meta, rows=n_meta, name="meta_layer")

    y_prompt, ca_p, cb_p, cf_p = _seq_layer(
        x_prompt, ca_m[0], cb_m[0], cf_m[0], wts, tile=512, rows=16, name="prompt_layer")
    new_a_p = ca_p[None, :, ha - hist_a:, :]
    new_b_p = cb_p[None, :, hb - 2:, :]
    new_f_p = _ff_unchunked_cols(cf_p[:, :, hb - 2:, :])[None]

    y_s, na_s, nb_s, nf_s = _sample_layer(
        x_sample.reshape(n_dec, dec_seq * d),
        state_conv_a[0].reshape(n_dec, hist_a * a_width),
        state_conv_b[0].reshape(n_dec, 2 * b_width),
        state_conv_ffn[0].reshape(n_dec, 2 * 2 * d_ff),
        wts, steps=dec_seq, seq_block=32)
    y_sample = y_s.reshape(n_dec, dec_seq, d)
    new_a_s = na_s.reshape(1, n_dec, hist_a, a_width)
    new_b_s = nb_s.reshape(1, n_dec, 2, b_width)
    new_f_s = nf_s.reshape(1, n_dec, 2, 2 * d_ff)

    return (y_prompt, y_sample, new_a_p, new_b_p, new_f_p, new_a_s, new_b_s, new_f_s)
```

```python
import functools

import jax
import jax.numpy as jnp
from jax import lax
from jax.experimental import pallas as pl
from jax.experimental.pallas import tpu as pltpu

EPS = 1e-6
HEAD_DIM = 64
LANES = 128
SUBLANES = 8
FF_CHUNK = 256
VMEM_LIMIT_BYTES = 56 * 1024 * 1024

_F32 = jnp.float32
_BF16 = jnp.bfloat16


def _rmsnorm(x, g):
    ms = jnp.mean(x * x, axis=-1, keepdims=True)
    return x * lax.rsqrt(ms + EPS) * g


def _sigmoid(x):
    return 1.0 / (1.0 + jnp.exp(-x))


def _silu(x):
    return x * _sigmoid(x)


def _group_mean(v, lane_lo):
    sa = jnp.sum(jnp.where(lane_lo, v, 0.0), axis=1, keepdims=True)
    sb = jnp.sum(jnp.where(lane_lo, 0.0, v), axis=1, keepdims=True)
    return jnp.where(lane_lo, sa, sb) * (1.0 / HEAD_DIM)


def _head_layernorm(ua):
    rows, width = ua.shape
    lane_lo = lax.broadcasted_iota(jnp.int32, (rows, LANES), 1) < HEAD_DIM
    outs = []
    for j in range(width // LANES):
        xb = ua[:, j * LANES:(j + 1) * LANES]
        d = xb - _group_mean(xb, lane_lo)
        var = _group_mean(d * d, lane_lo)
        outs.append(d * lax.rsqrt(var + EPS))
    return jnp.concatenate(outs, axis=1)


def _group_a_out(ua, lng, lnb, beta_a):
    return _silu(_head_layernorm(ua) * lng + lnb) * beta_a


def _slab_store(buf, row0, val):
    for c in range(buf.shape[0]):
        buf[c, pl.ds(row0, val.shape[0]), :] = val[:, c * LANES:(c + 1) * LANES]


def _slab_load(buf, row0, nrows):
    return jnp.concatenate(
        [buf[c, pl.ds(row0, nrows), :] for c in range(buf.shape[0])], axis=1)


def _slab_window(buf, row0, nrows):
    return jnp.concatenate(
        [buf[c, pl.ds(row0, nrows, stride=1), :] for c in range(buf.shape[0])], axis=1)


def _seq_kernel(x_ref, ca0_ref, cb0_ref, cf0_ref, g1_ref, win_ref, wa_ref, ba_ref,
                lng_ref, lnb_ref, wb_ref, beta_a_ref, beta_b_ref, wout_ref, g2_ref,
                wup_ref, wcf_ref, wdown_ref, gf_ref,
                y_ref, ca_ref, cb_ref, cf_ref,
                ubuf, zbuf, ucar, upbuf0, upbuf1, h_ref, proj_ref, mix_ref, x1_ref, act_ref,
                *, tile, norm_rows, mix_rows, ffn_rows, a_width, b_width, conv_a_w, n_chunks):
    t = pl.program_id(1)
    nt = pl.num_programs(1)
    ha = ubuf.shape[1] - tile
    hb = zbuf.shape[1] - tile
    off_a = ha - (conv_a_w - 1)

    @pl.when(t == 0)
    def _():
        _slab_store(ubuf, 0, ca0_ref[...])
        _slab_store(zbuf, 0, cb0_ref[...])
        ucar[...] = cf0_ref[...]

    def norm_rows_loop(src_ref, g_ref, dst_ref, dtype):
        def body(i, c):
            r0 = pl.multiple_of(i * norm_rows, norm_rows)
            x = src_ref[pl.ds(r0, norm_rows), :]
            dst_ref[pl.ds(r0, norm_rows), :] = _rmsnorm(x, g_ref[...]).astype(dtype)
            return c
        lax.fori_loop(0, tile // norm_rows, body, 0)

    norm_rows_loop(x_ref, g1_ref, h_ref, _BF16)

    proj_ref[...] = jnp.dot(h_ref[...], win_ref[...], preferred_element_type=_F32)

    def mixers(i, c):
        rows = mix_rows
        r0 = pl.multiple_of(i * rows, rows)
        p = lambda lo, w: proj_ref[pl.ds(r0, rows), lo:lo + w]
        u = p(0, a_width) * _sigmoid(p(a_width, a_width))
        _slab_store(ubuf, ha + r0, u)
        acc = jnp.broadcast_to(ba_ref[...], (rows, a_width))
        for k in range(conv_a_w):
            acc = acc + _slab_window(ubuf, r0 + off_a + k, rows) * wa_ref[k:k + 1, :]
        ya = _group_a_out(acc, lng_ref[...], lnb_ref[...], beta_a_ref[...])
        mix_ref[pl.ds(r0, rows), 0:a_width] = ya.astype(_BF16)

        o = 2 * a_width
        z = p(o + b_width, b_width) * p(o + 2 * b_width, b_width)
        _slab_store(zbuf, hb + r0, z)
        zb = (_slab_window(zbuf, r0 + hb - 2, rows) * wb_ref[0:1, :]
              + _slab_window(zbuf, r0 + hb - 1, rows) * wb_ref[1:2, :]
              + z * wb_ref[2:3, :])
        yb = p(o, b_width) * zb * beta_b_ref[...]
        mix_ref[pl.ds(r0, rows), a_width:a_width + b_width] = yb.astype(_BF16)
        return c
    lax.fori_loop(0, tile // mix_rows, mixers, 0)

    x1_ref[...] = x_ref[...] + jnp.dot(mix_ref[...], wout_ref[...],
                                       preferred_element_type=_F32)

    norm_rows_loop(x1_ref, g2_ref, h_ref, _BF16)

    upbufs = (upbuf0, upbuf1)

    def ffn_up(j, buf):
        _slab_store(buf, 0, ucar[j])
        _slab_store(buf, hb, jnp.dot(h_ref[...], wup_ref[j],
                                     preferred_element_type=_F32))
        ucar[j] = _slab_load(buf, tile, hb)

    def ffn_act(j, buf):
        w = wcf_ref[j]
        for r0 in range(0, tile, ffn_rows):
            cv = (_slab_window(buf, r0 + hb - 2, ffn_rows) * w[0:1, :]
                  + _slab_window(buf, r0 + hb - 1, ffn_rows) * w[1:2, :]
                  + _slab_load(buf, r0 + hb, ffn_rows) * w[2:3, :])
            a = _silu(cv[:, 0:FF_CHUNK]) * cv[:, FF_CHUNK:2 * FF_CHUNK]
            act_ref[j, r0:r0 + ffn_rows, :] = a.astype(_BF16)

    ffn_up(0, upbuf0)
    n_pairs = (n_chunks - 1) // 2

    def ffn_pair(k, c):
        j = 2 * k
        ffn_up(j + 1, upbuf1)
        ffn_act(j, upbuf0)
        ffn_up(j + 2, upbuf0)
        ffn_act(j + 1, upbuf1)
        return c
    lax.fori_loop(0, n_pairs, ffn_pair, 0)
    for j in range(2 * n_pairs + 1, n_chunks):
        ffn_up(j, upbufs[j % 2])
        ffn_act(j - 1, upbufs[(j - 1) % 2])
    ffn_act(n_chunks - 1, upbufs[(n_chunks - 1) % 2])

    acc = x1_ref[...]
    for j in range(n_chunks):
        acc = acc + jnp.dot(act_ref[j], wdown_ref[j], preferred_element_type=_F32)
    x1_ref[...] = acc

    norm_rows_loop(x1_ref, gf_ref, y_ref, _F32)

    _slab_store(ubuf, 0, _slab_load(ubuf, tile, ha))
    _slab_store(zbuf, 0, _slab_load(zbuf, tile, hb))

    @pl.when(t == nt - 1)
    def _():
        ca_ref[...] = _slab_load(ubuf, 0, ha)
        cb_ref[...] = _slab_load(zbuf, 0, hb)
        cf_ref[...] = ucar[...]


def _const_spec(shape):
    nd = len(shape)
    return pl.BlockSpec(shape, lambda b, t: (0,) * nd, pipeline_mode=pl.Buffered(1))


def _seq_layer(x, ca0, cb0, cf0, wts, *, tile, norm_rows, mix_rows, ffn_rows, name):
    nb, length, d = x.shape
    nt = length // tile
    (g1, win, wa, ba, lng, lnb, wb, beta_a, beta_b, wout, g2, wup, wcf, wdown, gf) = wts
    a_width = wa.shape[1]
    b_width = wb.shape[1]
    conv_a_w = wa.shape[0]
    n_chunks = wup.shape[0]
    ha, hb = ca0.shape[0], cb0.shape[0]
    kern = functools.partial(_seq_kernel, tile=tile, norm_rows=norm_rows, mix_rows=mix_rows,
                             ffn_rows=ffn_rows, a_width=a_width,
                             b_width=b_width, conv_a_w=conv_a_w, n_chunks=n_chunks)
    consts = (ca0, cb0, cf0) + tuple(wts)
    in_specs = [pl.BlockSpec((None, tile, d), lambda b, t: (b, t, 0))]
    in_specs += [_const_spec(c.shape) for c in consts]
    out_shape = (
        jax.ShapeDtypeStruct((nb, length, d), _F32),
        jax.ShapeDtypeStruct((nb, ha, a_width), _F32),
        jax.ShapeDtypeStruct((nb, hb, b_width), _F32),
        jax.ShapeDtypeStruct((nb,) + cf0.shape, _F32),
    )
    out_specs = (
        pl.BlockSpec((None, tile, d), lambda b, t: (b, t, 0)),
        pl.BlockSpec((None, ha, a_width), lambda b, t: (b, 0, 0)),
        pl.BlockSpec((None, hb, b_width), lambda b, t: (b, 0, 0)),
        pl.BlockSpec((None,) + cf0.shape, lambda b, t: (b, 0, 0, 0)),
    )
    scratch = [
        pltpu.VMEM((a_width // LANES, ha + tile, LANES), _F32),
        pltpu.VMEM((b_width // LANES, hb + tile, LANES), _F32),
        pltpu.VMEM(cf0.shape, _F32),
        pltpu.VMEM((2 * FF_CHUNK // LANES, hb + tile, LANES), _F32),
        pltpu.VMEM((2 * FF_CHUNK // LANES, hb + tile, LANES), _F32),
        pltpu.VMEM((tile, d), _BF16),
        pltpu.VMEM((tile, win.shape[1]), _F32),
        pltpu.VMEM((tile, a_width + b_width), _BF16),
        pltpu.VMEM((tile, d), _F32),
        pltpu.VMEM((n_chunks, tile, FF_CHUNK), _BF16),
    ]
    return pl.pallas_call(
        kern,
        grid=(nb, nt),
        in_specs=in_specs,
        out_specs=out_specs,
        out_shape=out_shape,
        scratch_shapes=scratch,
        compiler_params=pltpu.CompilerParams(
            dimension_semantics=("arbitrary", "arbitrary"),
            vmem_limit_bytes=VMEM_LIMIT_BYTES),
        name=name,
    )(x, *consts)


def _sample_kernel(xs_ref, sa_ref, sb_ref, sf_ref, g1_ref, win_ref, wa_ref, ba_ref,
                   lng_ref, lnb_ref, wb_ref, beta_a_ref, beta_b_ref, wout_ref, g2_ref,
                   wup_ref, wcf_ref, wdown_ref, gf_ref,
                   ys_ref, na_ref, nb_ref, nf_ref,
                   h_ref, proj_ref, u_ref, z_ref, mix_ref, x1_ref, up_ref, act_ref,
                   *, steps, d, a_width, b_width, conv_a_w, d_ff, n_chunks):
    s = xs_ref.shape[0]
    hist_a = conv_a_w - 1
    rows_of = lambda t: slice(t * s, (t + 1) * s)
    cols = lambda t, w: slice(t * w, (t + 1) * w)

    for t in range(steps):
        h_ref[rows_of(t), :] = _rmsnorm(xs_ref[:, cols(t, d)], g1_ref[...]).astype(_BF16)
    proj_ref[...] = jnp.dot(h_ref[...], win_ref[...], preferred_element_type=_F32)

    o = 2 * a_width
    for t in range(steps):
        r = rows_of(t)
        u_ref[r, :] = proj_ref[r, 0:a_width] * _sigmoid(proj_ref[r, a_width:o])
        z_ref[r, :] = (proj_ref[r, o + b_width:o + 2 * b_width]
                       * proj_ref[r, o + 2 * b_width:o + 3 * b_width])

    def xe_a(j):
        return sa_ref[:, cols(j, a_width)] if j < hist_a else u_ref[rows_of(j - hist_a), :]

    def xe_b(j):
        return sb_ref[:, cols(j, b_width)] if j < 2 else z_ref[rows_of(j - 2), :]

    for t in range(steps):
        r = rows_of(t)
        acc = jnp.broadcast_to(ba_ref[...], (s, a_width))
        for k in range(conv_a_w):
            acc = acc + xe_a(t + k) * wa_ref[k:k + 1, :]
        ya = _group_a_out(acc, lng_ref[...], lnb_ref[...], beta_a_ref[...])
        mix_ref[r, 0:a_width] = ya.astype(_BF16)
        zb = (xe_b(t) * wb_ref[0:1, :] + xe_b(t + 1) * wb_ref[1:2, :]
              + xe_b(t + 2) * wb_ref[2:3, :])
        yb = proj_ref[r, o:o + b_width] * zb * beta_b_ref[...]
        mix_ref[r, a_width:a_width + b_width] = yb.astype(_BF16)

    for j in range(hist_a):
        na_ref[:, cols(j, a_width)] = xe_a(j + steps)
    for j in range(2):
        nb_ref[:, cols(j, b_width)] = xe_b(j + steps)

    x1_ref[...] = jnp.dot(mix_ref[...], wout_ref[...], preferred_element_type=_F32)
    for t in range(steps):
        r = rows_of(t)
        x1 = x1_ref[r, :] + xs_ref[:, cols(t, d)]
        x1_ref[r, :] = x1
        h_ref[r, :] = _rmsnorm(x1, g2_ref[...]).astype(_BF16)

    for j in range(n_chunks):
        up_ref[...] = jnp.dot(h_ref[...], wup_ref[j], preferred_element_type=_F32)
        w = wcf_ref[j]
        lo = j * FF_CHUNK

        def xe_f(i, half):
            if i < 2:
                c0 = i * 2 * d_ff + half * d_ff + lo
                return sf_ref[:, c0:c0 + FF_CHUNK]
            return up_ref[rows_of(i - 2), half * FF_CHUNK:(half + 1) * FF_CHUNK]

        for t in range(steps):
            gv = []
            for half in range(2):
                wh = w[:, half * FF_CHUNK:(half + 1) * FF_CHUNK]
                gv.append(xe_f(t, half) * wh[0:1, :] + xe_f(t + 1, half) * wh[1:2, :]
                          + xe_f(t + 2, half) * wh[2:3, :])
            act_ref[j, rows_of(t), :] = (_silu(gv[0]) * gv[1]).astype(_BF16)
        for i in range(2):
            for half in range(2):
                c0 = i * 2 * d_ff + half * d_ff + lo
                nf_ref[:, c0:c0 + FF_CHUNK] = xe_f(i + steps, half)

    acc = x1_ref[...]
    for j in range(n_chunks):
        acc = acc + jnp.dot(act_ref[j], wdown_ref[j], preferred_element_type=_F32)
    for t in range(steps):
        ys_ref[:, cols(t, d)] = _rmsnorm(acc[rows_of(t), :], gf_ref[...])


def _sample_layer(xs, sa, sb, sf, wts, *, steps, seq_block):
    n, _ = xs.shape
    (g1, win, wa, ba, lng, lnb, wb, beta_a, beta_b, wout, g2, wup, wcf, wdown, gf) = wts
    d = g1.shape[1]
    a_width = wa.shape[1]
    b_width = wb.shape[1]
    conv_a_w = wa.shape[0]
    n_chunks = wup.shape[0]
    d_ff = n_chunks * FF_CHUNK
    m = steps * seq_block
    kern = functools.partial(_sample_kernel, steps=steps, d=d, a_width=a_width,
                             b_width=b_width, conv_a_w=conv_a_w, d_ff=d_ff,
                             n_chunks=n_chunks)
    row_spec = lambda a: pl.BlockSpec((seq_block, a.shape[1]), lambda i: (i, 0))
    const_spec = lambda a: pl.BlockSpec(a.shape, lambda i: (0,) * a.ndim,
                                        pipeline_mode=pl.Buffered(1))
    data = (xs, sa, sb, sf)
    in_specs = [row_spec(a) for a in data] + [const_spec(a) for a in wts]
    out_shape = tuple(jax.ShapeDtypeStruct(a.shape, _F32) for a in data)
    out_specs = tuple(row_spec(a) for a in data)
    scratch = [
        pltpu.VMEM((m, d), _BF16),
        pltpu.VMEM((m, win.shape[1]), _F32),
        pltpu.VMEM((m, a_width), _F32),
        pltpu.VMEM((m, b_width), _F32),
        pltpu.VMEM((m, a_width + b_width), _BF16),
        pltpu.VMEM((m, d), _F32),
        pltpu.VMEM((m, 2 * FF_CHUNK), _F32),
        pltpu.VMEM((n_chunks, m, FF_CHUNK), _BF16),
    ]
    return pl.pallas_call(
        kern,
        grid=(n // seq_block,),
        in_specs=in_specs,
        out_specs=out_specs,
        out_shape=out_shape,
        scratch_shapes=scratch,
        compiler_params=pltpu.CompilerParams(
            dimension_semantics=("arbitrary",),
            vmem_limit_bytes=VMEM_LIMIT_BYTES),
        name="sample_layer",
    )(*data, *wts)


def _ff_chunked_cols(a, d_ff):
    n_chunks = d_ff // FF_CHUNK
    lead = a.shape[:-1]
    a = a.reshape(lead + (2, n_chunks, FF_CHUNK))
    a = jnp.moveaxis(a, -2, 0)
    return a.reshape((n_chunks,) + lead + (2 * FF_CHUNK,))


def _ff_unchunked_cols(a):
    n_chunks, rows = a.shape[-3], a.shape[-2]
    lead = a.shape[:-3]
    a = a.reshape(lead + (n_chunks, rows, 2, FF_CHUNK))
    a = jnp.moveaxis(a, -4, -2)
    return a.reshape(lead + (rows, 2 * n_chunks * FF_CHUNK))


def kernel(x_prompt, x_sample, state_conv_a, state_conv_b, state_conv_ffn, meta_tokens,
           norm_mix_g, w_in, w_conv_a, b_conv_a, gn_a_g, gn_a_b, w_conv_b, beta_a, beta_b,
           w_out, norm_ffn_g, w_up, w_conv_f, w_down, norm_final_g):
    depth = w_in.shape[0]
    assert depth == 1, "single-layer step only"
    batch, seq, d = x_prompt.shape
    n_dec, dec_seq, _ = x_sample.shape
    n_meta = meta_tokens.shape[0]
    conv_a_w, a_width = w_conv_a.shape[1:]
    b_width = w_conv_b.shape[2]
    d_ff = w_down.shape[1]
    assert a_width % LANES == 0 and LANES == 2 * HEAD_DIM
    assert d_ff % FF_CHUNK == 0 and w_conv_b.shape[1] == 3 and w_conv_f.shape[1] == 3
    n_chunks = d_ff // FF_CHUNK
    hist_a = conv_a_w - 1
    ha = -(-hist_a // SUBLANES) * SUBLANES
    hb = SUBLANES

    row = lambda v: v.reshape(1, -1)
    wts = (
        row(norm_mix_g[0]),
        w_in[0].astype(_BF16),
        w_conv_a[0], row(b_conv_a[0]), row(gn_a_g[0]), row(gn_a_b[0]),
        w_conv_b[0], row(beta_a[0]), row(beta_b[0]),
        w_out[0].astype(_BF16),
        row(norm_ffn_g[0]),
        _ff_chunked_cols(w_up[0].astype(_BF16), d_ff),
        _ff_chunked_cols(w_conv_f[0], d_ff),
        w_down[0].astype(_BF16).reshape(n_chunks, FF_CHUNK, d),
        row(norm_final_g),
    )

    zeros = lambda *s: jnp.zeros(s, _F32)
    _, ca_m, cb_m, cf_m = _seq_layer(
        meta_tokens[None], zeros(ha, a_width), zeros(hb, b_width),
        zeros(n_chunks, hb, 2 * FF_CHUNK), wts, tile=n_meta, norm_rows=n_meta, mix_rows=n_meta,
        ffn_rows=n_meta, name="meta_layer")

    y_prompt, ca_p, cb_p, cf_p = _seq_layer(
        x_prompt, ca_m[0], cb_m[0], cf_m[0], wts, tile=512, norm_rows=64, mix_rows=32,
        ffn_rows=32, name="prompt_layer")
    new_a_p = ca_p[None, :, ha - hist_a:, :]
    new_b_p = cb_p[None, :, hb - 2:, :]
    new_f_p = _ff_unchunked_cols(cf_p[:, :, hb - 2:, :])[None]

    y_s, na_s, nb_s, nf_s = _sample_layer(
        x_sample.reshape(n_dec, dec_seq * d),
        state_conv_a[0].reshape(n_dec, hist_a * a_width),
        state_conv_b[0].reshape(n_dec, 2 * b_width),
        state_conv_ffn[0].reshape(n_dec, 2 * 2 * d_ff),
        wts, steps=dec_seq, seq_block=32)
    y_sample = y_s.reshape(n_dec, dec_seq, d)
    new_a_s = na_s.reshape(1, n_dec, hist_a, a_width)
    new_b_s = nb_s.reshape(1, n_dec, 2, b_width)
    new_f_s = nf_s.reshape(1, n_dec, 2, 2 * d_ff)

    return (y_prompt, y_sample, new_a_p, new_b_p, new_f_p, new_a_s, new_b_s, new_f_s)
```

```python
import functools

import jax
import jax.numpy as jnp
from jax import lax
from jax.experimental import pallas as pl
from jax.experimental.pallas import tpu as pltpu

EPS = 1e-6
HEAD_DIM = 64
LANES = 128
SUBLANES = 8
FF_CHUNK = 256
VMEM_LIMIT_BYTES = 56 * 1024 * 1024

_F32 = jnp.float32
_BF16 = jnp.bfloat16


def _rmsnorm(x, g):
    ms = jnp.mean(x * x, axis=-1, keepdims=True)
    return x * lax.rsqrt(ms + EPS) * g


def _sigmoid(x):
    return 1.0 / (1.0 + jnp.exp(-x))


def _silu(x):
    return x * _sigmoid(x)


def _group_mean(v, lane_lo):
    sa = jnp.sum(jnp.where(lane_lo, v, 0.0), axis=1, keepdims=True)
    sb = jnp.sum(jnp.where(lane_lo, 0.0, v), axis=1, keepdims=True)
    return jnp.where(lane_lo, sa, sb) * (1.0 / HEAD_DIM)


def _head_layernorm(ua):
    rows, width = ua.shape
    lane_lo = lax.broadcasted_iota(jnp.int32, (rows, LANES), 1) < HEAD_DIM
    outs = []
    for j in range(width // LANES):
        xb = ua[:, j * LANES:(j + 1) * LANES]
        d = xb - _group_mean(xb, lane_lo)
        var = _group_mean(d * d, lane_lo)
        outs.append(d * lax.rsqrt(var + EPS))
    return jnp.concatenate(outs, axis=1)


def _group_a_out(ua, lng, lnb, beta_a):
    return _silu(_head_layernorm(ua) * lng + lnb) * beta_a


def _slab_store(buf, row0, val):
    for c in range(buf.shape[0]):
        buf[c, pl.ds(row0, val.shape[0]), :] = val[:, c * LANES:(c + 1) * LANES]


def _slab_load(buf, row0, nrows):
    return jnp.concatenate(
        [buf[c, pl.ds(row0, nrows), :] for c in range(buf.shape[0])], axis=1)


def _slab_window(buf, row0, nrows):
    return jnp.concatenate(
        [buf[c, pl.ds(row0, nrows, stride=1), :] for c in range(buf.shape[0])], axis=1)


def _seq_kernel(x_ref, ca0_ref, cb0_ref, cf0_ref, g1_ref, win_ref, wa_ref, ba_ref,
                lng_ref, lnb_ref, wb_ref, beta_a_ref, beta_b_ref, wout_ref, g2_ref,
                wup_ref, wcf_ref, wdown_ref, gf_ref,
                y_ref, ca_ref, cb_ref, cf_ref,
                ubuf, zbuf, ucar, upbuf0, upbuf1, h1_ref, h2_ref, proj_ref, mix_ref, x1_ref,
                x2_ref, act_ref,
                *, tile, nt, n_tiles, norm_rows, mix_rows, mix_per_pair, ffn_rows, a_width,
                b_width, conv_a_w, n_chunks):
    s = pl.program_id(0)
    t_mix = lax.rem(s, nt)
    t_ffn = lax.rem(s + nt - 1, nt)
    has_mix = s < n_tiles
    has_ffn = s > 0
    ha = ubuf.shape[1] - tile
    hb = zbuf.shape[1] - tile
    off_a = ha - (conv_a_w - 1)

    @pl.when(s == 0)
    def _():
        h2_ref[...] = jnp.zeros(h2_ref.shape, h2_ref.dtype)
        x1_ref[...] = jnp.zeros(x1_ref.shape, x1_ref.dtype)

    @pl.when(t_mix == 0)
    def _():
        _slab_store(ubuf, 0, ca0_ref[...])
        _slab_store(zbuf, 0, cb0_ref[...])

    @pl.when(jnp.logical_or(s == 0, t_ffn == 0))
    def _():
        ucar[...] = cf0_ref[...]

    def norm_rows_static(src_ref, g_ref, dst_ref, dtype):
        for r0 in range(0, tile, norm_rows):
            x = src_ref[r0:r0 + norm_rows, :]
            dst_ref[r0:r0 + norm_rows, :] = _rmsnorm(x, g_ref[...]).astype(dtype)

    @pl.when(has_mix)
    def _():
        norm_rows_static(x_ref, g1_ref, h1_ref, _BF16)
        proj_ref[...] = jnp.dot(h1_ref[...], win_ref[...], preferred_element_type=_F32)

    def mixer_chunk(i):
        rows = mix_rows
        r0 = i * rows if isinstance(i, int) else pl.multiple_of(i * rows, rows)
        p = lambda lo, w: proj_ref[pl.ds(r0, rows), lo:lo + w]
        u = p(0, a_width) * _sigmoid(p(a_width, a_width))
        _slab_store(ubuf, ha + r0, u)
        acc = jnp.broadcast_to(ba_ref[...], (rows, a_width))
        for k in range(conv_a_w):
            acc = acc + _slab_window(ubuf, r0 + off_a + k, rows) * wa_ref[k:k + 1, :]
        ya = _group_a_out(acc, lng_ref[...], lnb_ref[...], beta_a_ref[...])
        mix_ref[pl.ds(r0, rows), 0:a_width] = ya.astype(_BF16)

        o = 2 * a_width
        z = p(o + b_width, b_width) * p(o + 2 * b_width, b_width)
        _slab_store(zbuf, hb + r0, z)
        zb = (_slab_window(zbuf, r0 + hb - 2, rows) * wb_ref[0:1, :]
              + _slab_window(zbuf, r0 + hb - 1, rows) * wb_ref[1:2, :]
              + z * wb_ref[2:3, :])
        yb = p(o, b_width) * zb * beta_b_ref[...]
        mix_ref[pl.ds(r0, rows), a_width:a_width + b_width] = yb.astype(_BF16)

    upbufs = (upbuf0, upbuf1)

    def ffn_up(j, buf):
        _slab_store(buf, 0, ucar[j])
        _slab_store(buf, hb, jnp.dot(h2_ref[...], wup_ref[j],
                                     preferred_element_type=_F32))
        ucar[j] = _slab_load(buf, tile, hb)

    def ffn_act(j, buf):
        w = wcf_ref[j]
        for r0 in range(0, tile, ffn_rows):
            cv = (_slab_window(buf, r0 + hb - 2, ffn_rows) * w[0:1, :]
                  + _slab_window(buf, r0 + hb - 1, ffn_rows) * w[1:2, :]
                  + _slab_load(buf, r0 + hb, ffn_rows) * w[2:3, :])
            a = _silu(cv[:, 0:FF_CHUNK]) * cv[:, FF_CHUNK:2 * FF_CHUNK]
            act_ref[j, r0:r0 + ffn_rows, :] = a.astype(_BF16)

    n_pairs = (n_chunks - 1) // 2
    n_mix = tile // mix_rows
    assert n_pairs * mix_per_pair <= n_mix

    ffn_up(0, upbuf0)

    def ffn_pair(k, c):
        j = 2 * k
        ffn_up(j + 1, upbuf1)
        ffn_act(j, upbuf0)
        ffn_up(j + 2, upbuf0)
        ffn_act(j + 1, upbuf1)
        for i in range(mix_per_pair):
            mixer_chunk(k * mix_per_pair + i)
        return c
    lax.fori_loop(0, n_pairs, ffn_pair, 0)
    for j in range(2 * n_pairs + 1, n_chunks):
        ffn_up(j, upbufs[j % 2])
        ffn_act(j - 1, upbufs[(j - 1) % 2])
    ffn_act(n_chunks - 1, upbufs[(n_chunks - 1) % 2])

    acc = x1_ref[...]
    for j in range(n_chunks):
        acc = acc + jnp.dot(act_ref[j], wdown_ref[j], preferred_element_type=_F32)
    x2_ref[...] = acc
    for i in range(n_pairs * mix_per_pair, n_mix):
        mixer_chunk(i)
    x1_ref[...] = x_ref[...] + jnp.dot(mix_ref[...], wout_ref[...],
                                       preferred_element_type=_F32)
    norm_rows_static(x2_ref, gf_ref, y_ref, _F32)
    norm_rows_static(x1_ref, g2_ref, h2_ref, _BF16)
    _slab_store(ubuf, 0, _slab_load(ubuf, tile, ha))
    _slab_store(zbuf, 0, _slab_load(zbuf, tile, hb))

    @pl.when(jnp.logical_and(has_ffn, t_ffn == nt - 1))
    def _():
        cf_ref[...] = ucar[...]

    @pl.when(jnp.logical_and(has_mix, t_mix == nt - 1))
    def _():
        ca_ref[...] = _slab_load(ubuf, 0, ha)
        cb_ref[...] = _slab_load(zbuf, 0, hb)


def _seq_layer(x, ca0, cb0, cf0, wts, *, tile, norm_rows, mix_rows, mix_per_pair, ffn_rows,
               name):
    nb, length, d = x.shape
    nt = length // tile
    n_tiles = nb * nt
    (g1, win, wa, ba, lng, lnb, wb, beta_a, beta_b, wout, g2, wup, wcf, wdown, gf) = wts
    a_width = wa.shape[1]
    b_width = wb.shape[1]
    conv_a_w = wa.shape[0]
    n_chunks = wup.shape[0]
    ha, hb = ca0.shape[0], cb0.shape[0]
    kern = functools.partial(_seq_kernel, tile=tile, nt=nt, n_tiles=n_tiles,
                             norm_rows=norm_rows, mix_rows=mix_rows,
                             mix_per_pair=mix_per_pair, ffn_rows=ffn_rows,
                             a_width=a_width, b_width=b_width, conv_a_w=conv_a_w,
                             n_chunks=n_chunks)
    mix_tile = lambda s: jnp.minimum(s, n_tiles - 1)
    ffn_tile = lambda s: jnp.maximum(s - 1, 0)
    const_spec = lambda a: pl.BlockSpec(a.shape, lambda s: (0,) * a.ndim,
                                        pipeline_mode=pl.Buffered(1))
    consts = (ca0, cb0, cf0) + tuple(wts)
    in_specs = [pl.BlockSpec((None, tile, d),
                             lambda s: (mix_tile(s) // nt, mix_tile(s) % nt, 0))]
    in_specs += [const_spec(c) for c in consts]
    out_shape = (
        jax.ShapeDtypeStruct((nb, length, d), _F32),
        jax.ShapeDtypeStruct((nb, ha, a_width), _F32),
        jax.ShapeDtypeStruct((nb, hb, b_width), _F32),
        jax.ShapeDtypeStruct((nb,) + cf0.shape, _F32),
    )
    out_specs = (
        pl.BlockSpec((None, tile, d), lambda s: (ffn_tile(s) // nt, ffn_tile(s) % nt, 0)),
        pl.BlockSpec((None, ha, a_width), lambda s: (mix_tile(s) // nt, 0, 0)),
        pl.BlockSpec((None, hb, b_width), lambda s: (mix_tile(s) // nt, 0, 0)),
        pl.BlockSpec((None,) + cf0.shape, lambda s: (ffn_tile(s) // nt, 0, 0, 0)),
    )
    scratch = [
        pltpu.VMEM((a_width // LANES, ha + tile, LANES), _F32),
        pltpu.VMEM((b_width // LANES, hb + tile, LANES), _F32),
        pltpu.VMEM(cf0.shape, _F32),
        pltpu.VMEM((2 * FF_CHUNK // LANES, hb + tile, LANES), _F32),
        pltpu.VMEM((2 * FF_CHUNK // LANES, hb + tile, LANES), _F32),
        pltpu.VMEM((tile, d), _BF16),
        pltpu.VMEM((tile, d), _BF16),
        pltpu.VMEM((tile, win.shape[1]), _F32),
        pltpu.VMEM((tile, a_width + b_width), _BF16),
        pltpu.VMEM((tile, d), _F32),
        pltpu.VMEM((tile, d), _F32),
        pltpu.VMEM((n_chunks, tile, FF_CHUNK), _BF16),
    ]
    return pl.pallas_call(
        kern,
        grid=(n_tiles + 1,),
        in_specs=in_specs,
        out_specs=out_specs,
        out_shape=out_shape,
        scratch_shapes=scratch,
        compiler_params=pltpu.CompilerParams(
            dimension_semantics=("arbitrary",),
            vmem_limit_bytes=VMEM_LIMIT_BYTES),
        name=name,
    )(x, *consts)


def _sample_kernel(xs_ref, sa_ref, sb_ref, sf_ref, g1_ref, win_ref, wa_ref, ba_ref,
                   lng_ref, lnb_ref, wb_ref, beta_a_ref, beta_b_ref, wout_ref, g2_ref,
                   wup_ref, wcf_ref, wdown_ref, gf_ref,
                   ys_ref, na_ref, nb_ref, nf_ref,
                   h_ref, proj_ref, u_ref, z_ref, mix_ref, x1_ref, up_ref, act_ref,
                   *, steps, d, a_width, b_width, conv_a_w, d_ff, n_chunks):
    s = xs_ref.shape[0]
    hist_a = conv_a_w - 1
    rows_of = lambda t: slice(t * s, (t + 1) * s)
    cols = lambda t, w: slice(t * w, (t + 1) * w)

    for t in range(steps):
        h_ref[rows_of(t), :] = _rmsnorm(xs_ref[:, cols(t, d)], g1_ref[...]).astype(_BF16)
    proj_ref[...] = jnp.dot(h_ref[...], win_ref[...], preferred_element_type=_F32)

    o = 2 * a_width
    for t in range(steps):
        r = rows_of(t)
        u_ref[r, :] = proj_ref[r, 0:a_width] * _sigmoid(proj_ref[r, a_width:o])
        z_ref[r, :] = (proj_ref[r, o + b_width:o + 2 * b_width]
                       * proj_ref[r, o + 2 * b_width:o + 3 * b_width])

    def xe_a(j):
        return sa_ref[:, cols(j, a_width)] if j < hist_a else u_ref[rows_of(j - hist_a), :]

    def xe_b(j):
        return sb_ref[:, cols(j, b_width)] if j < 2 else z_ref[rows_of(j - 2), :]

    for t in range(steps):
        r = rows_of(t)
        acc = jnp.broadcast_to(ba_ref[...], (s, a_width))
        for k in range(conv_a_w):
            acc = acc + xe_a(t + k) * wa_ref[k:k + 1, :]
        ya = _group_a_out(acc, lng_ref[...], lnb_ref[...], beta_a_ref[...])
        mix_ref[r, 0:a_width] = ya.astype(_BF16)
        zb = (xe_b(t) * wb_ref[0:1, :] + xe_b(t + 1) * wb_ref[1:2, :]
              + xe_b(t + 2) * wb_ref[2:3, :])
        yb = proj_ref[r, o:o + b_width] * zb * beta_b_ref[...]
        mix_ref[r, a_width:a_width + b_width] = yb.astype(_BF16)

    for j in range(hist_a):
        na_ref[:, cols(j, a_width)] = xe_a(j + steps)
    for j in range(2):
        nb_ref[:, cols(j, b_width)] = xe_b(j + steps)

    x1_ref[...] = jnp.dot(mix_ref[...], wout_ref[...], preferred_element_type=_F32)
    for t in range(steps):
        r = rows_of(t)
        x1 = x1_ref[r, :] + xs_ref[:, cols(t, d)]
        x1_ref[r, :] = x1
        h_ref[r, :] = _rmsnorm(x1, g2_ref[...]).astype(_BF16)

    for j in range(n_chunks):
        up_ref[...] = jnp.dot(h_ref[...], wup_ref[j], preferred_element_type=_F32)
        w = wcf_ref[j]
        lo = j * FF_CHUNK

        def xe_f(i, half):
            if i < 2:
                c0 = i * 2 * d_ff + half * d_ff + lo
                return sf_ref[:, c0:c0 + FF_CHUNK]
            return up_ref[rows_of(i - 2), half * FF_CHUNK:(half + 1) * FF_CHUNK]

        for t in range(steps):
            gv = []
            for half in range(2):
                wh = w[:, half * FF_CHUNK:(half + 1) * FF_CHUNK]
                gv.append(xe_f(t, half) * wh[0:1, :] + xe_f(t + 1, half) * wh[1:2, :]
                          + xe_f(t + 2, half) * wh[2:3, :])
            act_ref[j, rows_of(t), :] = (_silu(gv[0]) * gv[1]).astype(_BF16)
        for i in range(2):
            for half in range(2):
                c0 = i * 2 * d_ff + half * d_ff + lo
                nf_ref[:, c0:c0 + FF_CHUNK] = xe_f(i + steps, half)

    acc = x1_ref[...]
    for j in range(n_chunks):
        acc = acc + jnp.dot(act_ref[j], wdown_ref[j], preferred_element_type=_F32)
    for t in range(steps):
        ys_ref[:, cols(t, d)] = _rmsnorm(acc[rows_of(t), :], gf_ref[...])


def _sample_layer(xs, sa, sb, sf, wts, *, steps, seq_block):
    n, _ = xs.shape
    (g1, win, wa, ba, lng, lnb, wb, beta_a, beta_b, wout, g2, wup, wcf, wdown, gf) = wts
    d = g1.shape[1]
    a_width = wa.shape[1]
    b_width = wb.shape[1]
    conv_a_w = wa.shape[0]
    n_chunks = wup.shape[0]
    d_ff = n_chunks * FF_CHUNK
    m = steps * seq_block
    kern = functools.partial(_sample_kernel, steps=steps, d=d, a_width=a_width,
                             b_width=b_width, conv_a_w=conv_a_w, d_ff=d_ff,
                             n_chunks=n_chunks)
    row_spec = lambda a: pl.BlockSpec((seq_block, a.shape[1]), lambda i: (i, 0))
    const_spec = lambda a: pl.BlockSpec(a.shape, lambda i: (0,) * a.ndim,
                                        pipeline_mode=pl.Buffered(1))
    data = (xs, sa, sb, sf)
    in_specs = [row_spec(a) for a in data] + [const_spec(a) for a in wts]
    out_shape = tuple(jax.ShapeDtypeStruct(a.shape, _F32) for a in data)
    out_specs = tuple(row_spec(a) for a in data)
    scratch = [
        pltpu.VMEM((m, d), _BF16),
        pltpu.VMEM((m, win.shape[1]), _F32),
        pltpu.VMEM((m, a_width), _F32),
        pltpu.VMEM((m, b_width), _F32),
        pltpu.VMEM((m, a_width + b_width), _BF16),
        pltpu.VMEM((m, d), _F32),
        pltpu.VMEM((m, 2 * FF_CHUNK), _F32),
        pltpu.VMEM((n_chunks, m, FF_CHUNK), _BF16),
    ]
    return pl.pallas_call(
        kern,
        grid=(n // seq_block,),
        in_specs=in_specs,
        out_specs=out_specs,
        out_shape=out_shape,
        scratch_shapes=scratch,
        compiler_params=pltpu.CompilerParams(
            dimension_semantics=("arbitrary",),
            vmem_limit_bytes=VMEM_LIMIT_BYTES),
        name="sample_layer",
    )(*data, *wts)


def _ff_chunked_cols(a, d_ff):
    n_chunks = d_ff // FF_CHUNK
    lead = a.shape[:-1]
    a = a.reshape(lead + (2, n_chunks, FF_CHUNK))
    a = jnp.moveaxis(a, -2, 0)
    return a.reshape((n_chunks,) + lead + (2 * FF_CHUNK,))


def _ff_unchunked_cols(a):
    n_chunks, rows = a.shape[-3], a.shape[-2]
    lead = a.shape[:-3]
    a = a.reshape(lead + (n_chunks, rows, 2, FF_CHUNK))
    a = jnp.moveaxis(a, -4, -2)
    return a.reshape(lead + (rows, 2 * n_chunks * FF_CHUNK))


def kernel(x_prompt, x_sample, state_conv_a, state_conv_b, state_conv_ffn, meta_tokens,
           norm_mix_g, w_in, w_conv_a, b_conv_a, gn_a_g, gn_a_b, w_conv_b, beta_a, beta_b,
           w_out, norm_ffn_g, w_up, w_conv_f, w_down, norm_final_g):
    depth = w_in.shape[0]
    assert depth == 1, "single-layer step only"
    batch, seq, d = x_prompt.shape
    n_dec, dec_seq, _ = x_sample.shape
    n_meta = meta_tokens.shape[0]
    conv_a_w, a_width = w_conv_a.shape[1:]
    b_width = w_conv_b.shape[2]
    d_ff = w_down.shape[1]
    assert a_width % LANES == 0 and LANES == 2 * HEAD_DIM
    assert d_ff % FF_CHUNK == 0 and w_conv_b.shape[1] == 3 and w_conv_f.shape[1] == 3
    n_chunks = d_ff // FF_CHUNK
    hist_a = conv_a_w - 1
    ha = -(-hist_a // SUBLANES) * SUBLANES
    hb = SUBLANES

    row = lambda v: v.reshape(1, -1)
    wts = (
        row(norm_mix_g[0]),
        w_in[0].astype(_BF16),
        w_conv_a[0], row(b_conv_a[0]), row(gn_a_g[0]), row(gn_a_b[0]),
        w_conv_b[0], row(beta_a[0]), row(beta_b[0]),
        w_out[0].astype(_BF16),
        row(norm_ffn_g[0]),
        _ff_chunked_cols(w_up[0].astype(_BF16), d_ff),
        _ff_chunked_cols(w_conv_f[0], d_ff),
        w_down[0].astype(_BF16).reshape(n_chunks, FF_CHUNK, d),
        row(norm_final_g),
    )

    zeros = lambda *s: jnp.zeros(s, _F32)
    _, ca_m, cb_m, cf_m = _seq_layer(
        meta_tokens[None], zeros(ha, a_width), zeros(hb, b_width),
        zeros(n_chunks, hb, 2 * FF_CHUNK), wts, tile=n_meta, norm_rows=n_meta, mix_rows=n_meta,
        mix_per_pair=0, ffn_rows=n_meta, name="meta_layer")

    y_prompt, ca_p, cb_p, cf_p = _seq_layer(
        x_prompt, ca_m[0], cb_m[0], cf_m[0], wts, tile=512, norm_rows=64, mix_rows=32,
        mix_per_pair=2, ffn_rows=32, name="prompt_layer")
    new_a_p = ca_p[None, :, ha - hist_a:, :]
    new_b_p = cb_p[None, :, hb - 2:, :]
    new_f_p = _ff_unchunked_cols(cf_p[:, :, hb - 2:, :])[None]

    y_s, na_s, nb_s, nf_s = _sample_layer(
        x_sample.reshape(n_dec, dec_seq * d),
        state_conv_a[0].reshape(n_dec, hist_a * a_width),
        state_conv_b[0].reshape(n_dec, 2 * b_width),
        state_conv_ffn[0].reshape(n_dec, 2 * 2 * d_ff),
        wts, steps=dec_seq, seq_block=32)
    y_sample = y_s.reshape(n_dec, dec_seq, d)
    new_a_s = na_s.reshape(1, n_dec, hist_a, a_width)
    new_b_s = nb_s.reshape(1, n_dec, 2, b_width)
    new_f_s = nf_s.reshape(1, n_dec, 2, 2 * d_ff)

    return (y_prompt, y_sample, new_a_p, new_b_p, new_f_p, new_a_s, new_b_s, new_f_s)
```

```python
import functools

import jax
import jax.numpy as jnp
from jax import lax
from jax.experimental import pallas as pl
from jax.experimental.pallas import tpu as pltpu

EPS = 1e-6
HEAD_DIM = 64
LANES = 128
SUBLANES = 8
FF_CHUNK = 256
VMEM_LIMIT_BYTES = 56 * 1024 * 1024

_F32 = jnp.float32
_BF16 = jnp.bfloat16


def _rmsnorm(x, g):
    ms = jnp.mean(x * x, axis=-1, keepdims=True)
    return x * lax.rsqrt(ms + EPS) * g


def _sigmoid(x):
    return 1.0 / (1.0 + jnp.exp(-x))


def _silu(x):
    return x * _sigmoid(x)


def _group_mean(v, lane_lo):
    sa = jnp.sum(jnp.where(lane_lo, v, 0.0), axis=1, keepdims=True)
    sb = jnp.sum(jnp.where(lane_lo, 0.0, v), axis=1, keepdims=True)
    return jnp.where(lane_lo, sa, sb) * (1.0 / HEAD_DIM)


def _head_layernorm(ua):
    rows, width = ua.shape
    lane_lo = lax.broadcasted_iota(jnp.int32, (rows, LANES), 1) < HEAD_DIM
    outs = []
    for j in range(width // LANES):
        xb = ua[:, j * LANES:(j + 1) * LANES]
        d = xb - _group_mean(xb, lane_lo)
        var = _group_mean(d * d, lane_lo)
        outs.append(d * lax.rsqrt(var + EPS))
    return jnp.concatenate(outs, axis=1)


def _group_a_out(ua, lng, lnb, beta_a):
    return _silu(_head_layernorm(ua) * lng + lnb) * beta_a


def _ff_cols(ref, j, d_ff):
    off = j * FF_CHUNK
    if not isinstance(j, int):
        off = pl.multiple_of(off, FF_CHUNK)
    return jnp.concatenate([ref[:, pl.ds(off, FF_CHUNK)],
                            ref[:, pl.ds(d_ff + off, FF_CHUNK)]], axis=1)


def _ff_up_chunk(h, wup_ref, j, d_ff):
    off = j * FF_CHUNK
    if not isinstance(j, int):
        off = pl.multiple_of(off, FF_CHUNK)
    g = jnp.dot(h, wup_ref[:, pl.ds(off, FF_CHUNK)], preferred_element_type=_F32)
    v = jnp.dot(h, wup_ref[:, pl.ds(d_ff + off, FF_CHUNK)], preferred_element_type=_F32)
    return jnp.concatenate([g, v], axis=1)


def _ff_down_chain(acc, act_ref, wdown_ref, n_chunks):
    for j in range(n_chunks):
        acc = acc + jnp.dot(act_ref[j], wdown_ref[j * FF_CHUNK:(j + 1) * FF_CHUNK, :],
                            preferred_element_type=_F32)
    return acc


def _slab_store(buf, row0, val):
    for c in range(buf.shape[0]):
        buf[c, pl.ds(row0, val.shape[0]), :] = val[:, c * LANES:(c + 1) * LANES]


def _slab_load(buf, row0, nrows):
    return jnp.concatenate(
        [buf[c, pl.ds(row0, nrows), :] for c in range(buf.shape[0])], axis=1)


def _slab_window(buf, row0, nrows):
    return jnp.concatenate(
        [buf[c, pl.ds(row0, nrows, stride=1), :] for c in range(buf.shape[0])], axis=1)


def _seq_kernel(x_ref, ca0_ref, cb0_ref, cf0_ref, g1_ref, win_ref, wa_ref, ba_ref,
                lng_ref, lnb_ref, wb_ref, beta_a_ref, beta_b_ref, wout_ref, g2_ref,
                wup_ref, wcf_ref, wdown_ref, gf_ref,
                y_ref, ca_ref, cb_ref, cf_ref,
                ubuf, zbuf, ucar, upbuf0, upbuf1, h1_ref, h2_ref, proj_ref, mix_ref, x1_ref,
                x2_ref, act_ref,
                *, tile, nt, n_tiles, norm_rows, mix_rows, mix_per_pair, ffn_rows, a_width,
                b_width, conv_a_w, d_ff):
    n_chunks = d_ff // FF_CHUNK
    s = pl.program_id(0)
    t_mix = lax.rem(s, nt)
    t_ffn = lax.rem(s + nt - 1, nt)
    has_mix = s < n_tiles
    has_ffn = s > 0
    ha = ubuf.shape[1] - tile
    hb = zbuf.shape[1] - tile
    off_a = ha - (conv_a_w - 1)

    @pl.when(s == 0)
    def _():
        h2_ref[...] = jnp.zeros(h2_ref.shape, h2_ref.dtype)
        x1_ref[...] = jnp.zeros(x1_ref.shape, x1_ref.dtype)

    @pl.when(t_mix == 0)
    def _():
        _slab_store(ubuf, 0, ca0_ref[...])
        _slab_store(zbuf, 0, cb0_ref[...])

    @pl.when(jnp.logical_or(s == 0, t_ffn == 0))
    def _():
        ucar[...] = cf0_ref[...]

    def norm_rows_static(src_ref, g_ref, dst_ref, dtype):
        for r0 in range(0, tile, norm_rows):
            x = src_ref[r0:r0 + norm_rows, :]
            dst_ref[r0:r0 + norm_rows, :] = _rmsnorm(x, g_ref[...]).astype(dtype)

    @pl.when(has_mix)
    def _():
        norm_rows_static(x_ref, g1_ref, h1_ref, _BF16)
        proj_ref[...] = jnp.dot(h1_ref[...], win_ref[...], preferred_element_type=_F32)

    def mixer_chunk(i):
        rows = mix_rows
        r0 = i * rows if isinstance(i, int) else pl.multiple_of(i * rows, rows)
        p = lambda lo, w: proj_ref[pl.ds(r0, rows), lo:lo + w]
        u = p(0, a_width) * _sigmoid(p(a_width, a_width))
        _slab_store(ubuf, ha + r0, u)
        acc = jnp.broadcast_to(ba_ref[...], (rows, a_width))
        for k in range(conv_a_w):
            acc = acc + _slab_window(ubuf, r0 + off_a + k, rows) * wa_ref[k:k + 1, :]
        ya = _group_a_out(acc, lng_ref[...], lnb_ref[...], beta_a_ref[...])
        mix_ref[pl.ds(r0, rows), 0:a_width] = ya.astype(_BF16)

        o = 2 * a_width
        z = p(o + b_width, b_width) * p(o + 2 * b_width, b_width)
        _slab_store(zbuf, hb + r0, z)
        zb = (_slab_window(zbuf, r0 + hb - 2, rows) * wb_ref[0:1, :]
              + _slab_window(zbuf, r0 + hb - 1, rows) * wb_ref[1:2, :]
              + z * wb_ref[2:3, :])
        yb = p(o, b_width) * zb * beta_b_ref[...]
        mix_ref[pl.ds(r0, rows), a_width:a_width + b_width] = yb.astype(_BF16)

    upbufs = (upbuf0, upbuf1)

    def ffn_up(j, buf):
        _slab_store(buf, 0, ucar[j])
        _slab_store(buf, hb, _ff_up_chunk(h2_ref[...], wup_ref, j, d_ff))
        ucar[j] = _slab_load(buf, tile, hb)

    def ffn_act(j, buf):
        w = _ff_cols(wcf_ref, j, d_ff)
        for r0 in range(0, tile, ffn_rows):
            cv = (_slab_window(buf, r0 + hb - 2, ffn_rows) * w[0:1, :]
                  + _slab_window(buf, r0 + hb - 1, ffn_rows) * w[1:2, :]
                  + _slab_load(buf, r0 + hb, ffn_rows) * w[2:3, :])
            a = _silu(cv[:, 0:FF_CHUNK]) * cv[:, FF_CHUNK:2 * FF_CHUNK]
            act_ref[j, r0:r0 + ffn_rows, :] = a.astype(_BF16)

    n_pairs = (n_chunks - 1) // 2
    n_mix = tile // mix_rows
    assert n_pairs * mix_per_pair <= n_mix

    ffn_up(0, upbuf0)

    def ffn_pair(k, c):
        j = 2 * k
        ffn_up(j + 1, upbuf1)
        ffn_act(j, upbuf0)
        ffn_up(j + 2, upbuf0)
        ffn_act(j + 1, upbuf1)
        for i in range(mix_per_pair):
            mixer_chunk(k * mix_per_pair + i)
        return c
    lax.fori_loop(0, n_pairs, ffn_pair, 0)
    for j in range(2 * n_pairs + 1, n_chunks):
        ffn_up(j, upbufs[j % 2])
        ffn_act(j - 1, upbufs[(j - 1) % 2])
    ffn_act(n_chunks - 1, upbufs[(n_chunks - 1) % 2])

    x2_ref[...] = _ff_down_chain(x1_ref[...], act_ref, wdown_ref, n_chunks)
    for i in range(n_pairs * mix_per_pair, n_mix):
        mixer_chunk(i)
    x1_ref[...] = x_ref[...] + jnp.dot(mix_ref[...], wout_ref[...],
                                       preferred_element_type=_F32)
    norm_rows_static(x2_ref, gf_ref, y_ref, _F32)
    norm_rows_static(x1_ref, g2_ref, h2_ref, _BF16)
    _slab_store(ubuf, 0, _slab_load(ubuf, tile, ha))
    _slab_store(zbuf, 0, _slab_load(zbuf, tile, hb))

    @pl.when(jnp.logical_and(has_ffn, t_ffn == nt - 1))
    def _():
        cf_ref[...] = ucar[...]

    @pl.when(jnp.logical_and(has_mix, t_mix == nt - 1))
    def _():
        ca_ref[...] = _slab_load(ubuf, 0, ha)
        cb_ref[...] = _slab_load(zbuf, 0, hb)


def _seq_layer(x, ca0, cb0, cf0, wts, *, tile, norm_rows, mix_rows, mix_per_pair, ffn_rows,
               name):
    nb, length, d = x.shape
    nt = length // tile
    n_tiles = nb * nt
    (g1, win, wa, ba, lng, lnb, wb, beta_a, beta_b, wout, g2, wup, wcf, wdown, gf) = wts
    a_width = wa.shape[1]
    b_width = wb.shape[1]
    conv_a_w = wa.shape[0]
    d_ff = wdown.shape[0]
    n_chunks = d_ff // FF_CHUNK
    ha, hb = ca0.shape[0], cb0.shape[0]
    kern = functools.partial(_seq_kernel, tile=tile, nt=nt, n_tiles=n_tiles,
                             norm_rows=norm_rows, mix_rows=mix_rows,
                             mix_per_pair=mix_per_pair, ffn_rows=ffn_rows,
                             a_width=a_width, b_width=b_width, conv_a_w=conv_a_w,
                             d_ff=d_ff)
    mix_tile = lambda s: jnp.minimum(s, n_tiles - 1)
    ffn_tile = lambda s: jnp.maximum(s - 1, 0)
    const_spec = lambda a: pl.BlockSpec(a.shape, lambda s: (0,) * a.ndim,
                                        pipeline_mode=pl.Buffered(1))
    consts = (ca0, cb0, cf0) + tuple(wts)
    in_specs = [pl.BlockSpec((None, tile, d),
                             lambda s: (mix_tile(s) // nt, mix_tile(s) % nt, 0))]
    in_specs += [const_spec(c) for c in consts]
    out_shape = (
        jax.ShapeDtypeStruct((nb, length, d), _F32),
        jax.ShapeDtypeStruct((nb, ha, a_width), _F32),
        jax.ShapeDtypeStruct((nb, hb, b_width), _F32),
        jax.ShapeDtypeStruct((nb,) + cf0.shape, _F32),
    )
    out_specs = (
        pl.BlockSpec((None, tile, d), lambda s: (ffn_tile(s) // nt, ffn_tile(s) % nt, 0)),
        pl.BlockSpec((None, ha, a_width), lambda s: (mix_tile(s) // nt, 0, 0)),
        pl.BlockSpec((None, hb, b_width), lambda s: (mix_tile(s) // nt, 0, 0)),
        pl.BlockSpec((None,) + cf0.shape, lambda s: (ffn_tile(s) // nt, 0, 0, 0)),
    )
    scratch = [
        pltpu.VMEM((a_width // LANES, ha + tile, LANES), _F32),
        pltpu.VMEM((b_width // LANES, hb + tile, LANES), _F32),
        pltpu.VMEM(cf0.shape, _F32),
        pltpu.VMEM((2 * FF_CHUNK // LANES, hb + tile, LANES), _F32),
        pltpu.VMEM((2 * FF_CHUNK // LANES, hb + tile, LANES), _F32),
        pltpu.VMEM((tile, d), _BF16),
        pltpu.VMEM((tile, d), _BF16),
        pltpu.VMEM((tile, win.shape[1]), _F32),
        pltpu.VMEM((tile, a_width + b_width), _BF16),
        pltpu.VMEM((tile, d), _F32),
        pltpu.VMEM((tile, d), _F32),
        pltpu.VMEM((n_chunks, tile, FF_CHUNK), _BF16),
    ]
    return pl.pallas_call(
        kern,
        grid=(n_tiles + 1,),
        in_specs=in_specs,
        out_specs=out_specs,
        out_shape=out_shape,
        scratch_shapes=scratch,
        compiler_params=pltpu.CompilerParams(
            dimension_semantics=("arbitrary",),
            vmem_limit_bytes=VMEM_LIMIT_BYTES),
        name=name,
    )(x, *consts)


def _sample_kernel(xs_ref, sa_ref, sb_ref, sf_ref, g1_ref, win_ref, wa_ref, ba_ref,
                   lng_ref, lnb_ref, wb_ref, beta_a_ref, beta_b_ref, wout_ref, g2_ref,
                   wup_ref, wcf_ref, wdown_ref, gf_ref,
                   ys_ref, na_ref, nb_ref, nf_ref,
                   h_ref, proj_ref, u_ref, z_ref, mix_ref, x1_ref, up_ref, act_ref,
                   *, steps, d, a_width, b_width, conv_a_w, d_ff, n_chunks):
    s = xs_ref.shape[0]
    hist_a = conv_a_w - 1
    rows_of = lambda t: slice(t * s, (t + 1) * s)
    cols = lambda t, w: slice(t * w, (t + 1) * w)

    for t in range(steps):
        h_ref[rows_of(t), :] = _rmsnorm(xs_ref[:, cols(t, d)], g1_ref[...]).astype(_BF16)
    proj_ref[...] = jnp.dot(h_ref[...], win_ref[...], preferred_element_type=_F32)

    o = 2 * a_width
    for t in range(steps):
        r = rows_of(t)
        u_ref[r, :] = proj_ref[r, 0:a_width] * _sigmoid(proj_ref[r, a_width:o])
        z_ref[r, :] = (proj_ref[r, o + b_width:o + 2 * b_width]
                       * proj_ref[r, o + 2 * b_width:o + 3 * b_width])

    def xe_a(j):
        return sa_ref[:, cols(j, a_width)] if j < hist_a else u_ref[rows_of(j - hist_a), :]

    def xe_b(j):
        return sb_ref[:, cols(j, b_width)] if j < 2 else z_ref[rows_of(j - 2), :]

    for t in range(steps):
        r = rows_of(t)
        acc = jnp.broadcast_to(ba_ref[...], (s, a_width))
        for k in range(conv_a_w):
            acc = acc + xe_a(t + k) * wa_ref[k:k + 1, :]
        ya = _group_a_out(acc, lng_ref[...], lnb_ref[...], beta_a_ref[...])
        mix_ref[r, 0:a_width] = ya.astype(_BF16)
        zb = (xe_b(t) * wb_ref[0:1, :] + xe_b(t + 1) * wb_ref[1:2, :]
              + xe_b(t + 2) * wb_ref[2:3, :])
        yb = proj_ref[r, o:o + b_width] * zb * beta_b_ref[...]
        mix_ref[r, a_width:a_width + b_width] = yb.astype(_BF16)

    for j in range(hist_a):
        na_ref[:, cols(j, a_width)] = xe_a(j + steps)
    for j in range(2):
        nb_ref[:, cols(j, b_width)] = xe_b(j + steps)

    x1_ref[...] = jnp.dot(mix_ref[...], wout_ref[...], preferred_element_type=_F32)
    for t in range(steps):
        r = rows_of(t)
        x1 = x1_ref[r, :] + xs_ref[:, cols(t, d)]
        x1_ref[r, :] = x1
        h_ref[r, :] = _rmsnorm(x1, g2_ref[...]).astype(_BF16)

    for j in range(n_chunks):
        up_ref[...] = _ff_up_chunk(h_ref[...], wup_ref, j, d_ff)
        w = _ff_cols(wcf_ref, j, d_ff)
        lo = j * FF_CHUNK

        def xe_f(i, half):
            if i < 2:
                c0 = i * 2 * d_ff + half * d_ff + lo
                return sf_ref[:, c0:c0 + FF_CHUNK]
            return up_ref[rows_of(i - 2), half * FF_CHUNK:(half + 1) * FF_CHUNK]

        for t in range(steps):
            gv = []
            for half in range(2):
                wh = w[:, half * FF_CHUNK:(half + 1) * FF_CHUNK]
                gv.append(xe_f(t, half) * wh[0:1, :] + xe_f(t + 1, half) * wh[1:2, :]
                          + xe_f(t + 2, half) * wh[2:3, :])
            act_ref[j, rows_of(t), :] = (_silu(gv[0]) * gv[1]).astype(_BF16)
        for i in range(2):
            for half in range(2):
                c0 = i * 2 * d_ff + half * d_ff + lo
                nf_ref[:, c0:c0 + FF_CHUNK] = xe_f(i + steps, half)

    acc = _ff_down_chain(x1_ref[...], act_ref, wdown_ref, n_chunks)
    for t in range(steps):
        ys_ref[:, cols(t, d)] = _rmsnorm(acc[rows_of(t), :], gf_ref[...])


def _sample_layer(xs, sa, sb, sf, wts, *, steps, seq_block):
    n, _ = xs.shape
    (g1, win, wa, ba, lng, lnb, wb, beta_a, beta_b, wout, g2, wup, wcf, wdown, gf) = wts
    d = g1.shape[1]
    a_width = wa.shape[1]
    b_width = wb.shape[1]
    conv_a_w = wa.shape[0]
    d_ff = wdown.shape[0]
    n_chunks = d_ff // FF_CHUNK
    m = steps * seq_block
    kern = functools.partial(_sample_kernel, steps=steps, d=d, a_width=a_width,
                             b_width=b_width, conv_a_w=conv_a_w, d_ff=d_ff,
                             n_chunks=n_chunks)
    row_spec = lambda a: pl.BlockSpec((seq_block, a.shape[1]), lambda i: (i, 0))
    const_spec = lambda a: pl.BlockSpec(a.shape, lambda i: (0,) * a.ndim,
                                        pipeline_mode=pl.Buffered(1))
    data = (xs, sa, sb, sf)
    in_specs = [row_spec(a) for a in data] + [const_spec(a) for a in wts]
    out_shape = tuple(jax.ShapeDtypeStruct(a.shape, _F32) for a in data)
    out_specs = tuple(row_spec(a) for a in data)
    scratch = [
        pltpu.VMEM((m, d), _BF16),
        pltpu.VMEM((m, win.shape[1]), _F32),
        pltpu.VMEM((m, a_width), _F32),
        pltpu.VMEM((m, b_width), _F32),
        pltpu.VMEM((m, a_width + b_width), _BF16),
        pltpu.VMEM((m, d), _F32),
        pltpu.VMEM((m, 2 * FF_CHUNK), _F32),
        pltpu.VMEM((n_chunks, m, FF_CHUNK), _BF16),
    ]
    return pl.pallas_call(
        kern,
        grid=(n // seq_block,),
        in_specs=in_specs,
        out_specs=out_specs,
        out_shape=out_shape,
        scratch_shapes=scratch,
        compiler_params=pltpu.CompilerParams(
            dimension_semantics=("arbitrary",),
            vmem_limit_bytes=VMEM_LIMIT_BYTES),
        name="sample_layer",
    )(*data, *wts)


def _ff_unchunked_cols(a):
    n_chunks, rows = a.shape[-3], a.shape[-2]
    lead = a.shape[:-3]
    a = a.reshape(lead + (n_chunks, rows, 2, FF_CHUNK))
    a = jnp.moveaxis(a, -4, -2)
    return a.reshape(lead + (rows, 2 * n_chunks * FF_CHUNK))


def kernel(x_prompt, x_sample, state_conv_a, state_conv_b, state_conv_ffn, meta_tokens,
           norm_mix_g, w_in, w_conv_a, b_conv_a, gn_a_g, gn_a_b, w_conv_b, beta_a, beta_b,
           w_out, norm_ffn_g, w_up, w_conv_f, w_down, norm_final_g):
    depth = w_in.shape[0]
    assert depth == 1, "single-layer step only"
    batch, seq, d = x_prompt.shape
    n_dec, dec_seq, _ = x_sample.shape
    n_meta = meta_tokens.shape[0]
    conv_a_w, a_width = w_conv_a.shape[1:]
    b_width = w_conv_b.shape[2]
    d_ff = w_down.shape[1]
    assert a_width % LANES == 0 and LANES == 2 * HEAD_DIM
    assert d_ff % FF_CHUNK == 0 and w_conv_b.shape[1] == 3 and w_conv_f.shape[1] == 3
    n_chunks = d_ff // FF_CHUNK
    hist_a = conv_a_w - 1
    ha = -(-hist_a // SUBLANES) * SUBLANES
    hb = SUBLANES

    row = lambda v: v.reshape(1, -1)
    wts = (
        row(norm_mix_g[0]),
        w_in[0].astype(_BF16),
        w_conv_a[0], row(b_conv_a[0]), row(gn_a_g[0]), row(gn_a_b[0]),
        w_conv_b[0], row(beta_a[0]), row(beta_b[0]),
        w_out[0].astype(_BF16),
        row(norm_ffn_g[0]),
        w_up[0].astype(_BF16),
        w_conv_f[0],
        w_down[0].astype(_BF16),
        row(norm_final_g),
    )

    zeros = lambda *s: jnp.zeros(s, _F32)
    _, ca_m, cb_m, cf_m = _seq_layer(
        meta_tokens[None], zeros(ha, a_width), zeros(hb, b_width),
        zeros(n_chunks, hb, 2 * FF_CHUNK), wts, tile=n_meta, norm_rows=n_meta, mix_rows=n_meta,
        mix_per_pair=0, ffn_rows=n_meta, name="meta_layer")

    y_prompt, ca_p, cb_p, cf_p = _seq_layer(
        x_prompt, ca_m[0], cb_m[0], cf_m[0], wts, tile=512, norm_rows=64, mix_rows=32,
        mix_per_pair=2, ffn_rows=32, name="prompt_layer")
    new_a_p = ca_p[None, :, ha - hist_a:, :]
    new_b_p = cb_p[None, :, hb - 2:, :]
    new_f_p = _ff_unchunked_cols(cf_p[:, :, hb - 2:, :])[None]

    y_s, na_s, nb_s, nf_s = _sample_layer(
        x_sample.reshape(n_dec, dec_seq * d),
        state_conv_a[0].reshape(n_dec, hist_a * a_width),
        state_conv_b[0].reshape(n_dec, 2 * b_width),
        state_conv_ffn[0].reshape(n_dec, 2 * 2 * d_ff),
        wts, steps=dec_seq, seq_block=32)
    y_sample = y_s.reshape(n_dec, dec_seq, d)
    new_a_s = na_s.reshape(1, n_dec, hist_a, a_width)
    new_b_s = nb_s.reshape(1, n_dec, 2, b_width)
    new_f_s = nf_s.reshape(1, n_dec, 2, 2 * d_ff)

    return (y_prompt, y_sample, new_a_p, new_b_p, new_f_p, new_a_s, new_b_s, new_f_s)
```

```python
import functools

import jax
import jax.numpy as jnp
from jax import lax
from jax.experimental import pallas as pl
from jax.experimental.pallas import tpu as pltpu

EPS = 1e-6
HEAD_DIM = 64
LANES = 128
SUBLANES = 8
FF_CHUNK = 256
VMEM_LIMIT_BYTES = 56 * 1024 * 1024

_F32 = jnp.float32
_BF16 = jnp.bfloat16


def _rmsnorm(x, g):
    ms = jnp.mean(x * x, axis=-1, keepdims=True)
    return x * lax.rsqrt(ms + EPS) * g


def _sigmoid(x):
    return 1.0 / (1.0 + jnp.exp(-x))


def _silu(x):
    return x * _sigmoid(x)


def _group_mean(v, lane_lo):
    sa = jnp.sum(jnp.where(lane_lo, v, 0.0), axis=1, keepdims=True)
    sb = jnp.sum(jnp.where(lane_lo, 0.0, v), axis=1, keepdims=True)
    return jnp.where(lane_lo, sa, sb) * (1.0 / HEAD_DIM)


def _head_layernorm(ua):
    rows, width = ua.shape
    lane_lo = lax.broadcasted_iota(jnp.int32, (rows, LANES), 1) < HEAD_DIM
    outs = []
    for j in range(width // LANES):
        xb = ua[:, j * LANES:(j + 1) * LANES]
        d = xb - _group_mean(xb, lane_lo)
        var = _group_mean(d * d, lane_lo)
        outs.append(d * lax.rsqrt(var + EPS))
    return jnp.concatenate(outs, axis=1)


def _group_a_out(ua, lng, lnb, beta_a):
    return _silu(_head_layernorm(ua) * lng + lnb) * beta_a


def _ff_cols(ref, j, d_ff):
    off = j * FF_CHUNK
    if not isinstance(j, int):
        off = pl.multiple_of(off, FF_CHUNK)
    return jnp.concatenate([ref[:, pl.ds(off, FF_CHUNK)],
                            ref[:, pl.ds(d_ff + off, FF_CHUNK)]], axis=1)


def _ff_up_chunk(h, wup_ref, j, d_ff):
    off = j * FF_CHUNK
    if not isinstance(j, int):
        off = pl.multiple_of(off, FF_CHUNK)
    g = jnp.dot(h, wup_ref[:, pl.ds(off, FF_CHUNK)], preferred_element_type=_F32)
    v = jnp.dot(h, wup_ref[:, pl.ds(d_ff + off, FF_CHUNK)], preferred_element_type=_F32)
    return jnp.concatenate([g, v], axis=1)


def _ff_down_chain(acc, act_ref, wdown_ref, n_chunks):
    for j in range(n_chunks):
        acc = acc + jnp.dot(act_ref[j], wdown_ref[j * FF_CHUNK:(j + 1) * FF_CHUNK, :],
                            preferred_element_type=_F32)
    return acc


def _slab_store(buf, row0, val):
    for c in range(buf.shape[0]):
        buf[c, pl.ds(row0, val.shape[0]), :] = val[:, c * LANES:(c + 1) * LANES]


def _slab_load(buf, row0, nrows):
    return jnp.concatenate(
        [buf[c, pl.ds(row0, nrows), :] for c in range(buf.shape[0])], axis=1)


def _slab_window(buf, row0, nrows):
    return jnp.concatenate(
        [buf[c, pl.ds(row0, nrows, stride=1), :] for c in range(buf.shape[0])], axis=1)


def _seq_kernel(x_ref, ca0_ref, cb0_ref, cf0_ref, g1_ref, win_ref, wa_ref, ba_ref,
                lng_ref, lnb_ref, wb_ref, beta_a_ref, beta_b_ref, wout_ref, g2_ref,
                wup_ref, wcf_ref, wdown_ref, gf_ref,
                y_ref, ca_ref, cb_ref, cf_ref,
                ubuf, zbuf, ucar, upbuf0, upbuf1, h1_ref, h2_ref, proj_ref, mix_ref, x1_ref,
                x2_ref, act_ref,
                *, tile, nt, n_tiles, norm_rows, mix_rows, mix_first, mix_per_pair, ffn_rows,
                a_width, b_width, conv_a_w, d_ff, exact_states):
    n_chunks = d_ff // FF_CHUNK
    s = pl.program_id(0)
    t_mix = lax.rem(s, nt)
    t_ffn = lax.rem(s + nt - 1, nt)
    has_mix = s < n_tiles
    has_ffn = s > 0
    ha = ubuf.shape[1] - tile
    hb = zbuf.shape[1] - tile
    off_a = ha - (conv_a_w - 1)

    @pl.when(s == 0)
    def _():
        h2_ref[...] = jnp.zeros(h2_ref.shape, h2_ref.dtype)
        x1_ref[...] = jnp.zeros(x1_ref.shape, x1_ref.dtype)

    @pl.when(t_mix == 0)
    def _():
        _slab_store(ubuf, 0, ca0_ref[...])
        _slab_store(zbuf, 0, cb0_ref[...])

    @pl.when(jnp.logical_or(s == 0, t_ffn == 0))
    def _():
        ucar[...] = cf0_ref[...]

    def norm_rows_static(src_ref, g_ref, dst_ref, dtype):
        for r0 in range(0, tile, norm_rows):
            x = src_ref[r0:r0 + norm_rows, :]
            dst_ref[r0:r0 + norm_rows, :] = _rmsnorm(x, g_ref[...]).astype(dtype)

    @pl.when(has_mix)
    def _():
        norm_rows_static(x_ref, g1_ref, h1_ref, _BF16)
        proj_ref[...] = jnp.dot(h1_ref[...], win_ref[...], preferred_element_type=_F32)

    def mixer_chunk(i):
        rows = mix_rows
        r0 = i * rows if isinstance(i, int) else pl.multiple_of(i * rows, rows)
        p = lambda lo, w: proj_ref[pl.ds(r0, rows), lo:lo + w]
        u = p(0, a_width) * _sigmoid(p(a_width, a_width))
        _slab_store(ubuf, ha + r0, u)
        acc = jnp.broadcast_to(ba_ref[...], (rows, a_width))
        for k in range(conv_a_w):
            acc = acc + _slab_window(ubuf, r0 + off_a + k, rows) * wa_ref[k:k + 1, :]
        ya = _group_a_out(acc, lng_ref[...], lnb_ref[...], beta_a_ref[...])
        mix_ref[pl.ds(r0, rows), 0:a_width] = ya.astype(_BF16)

        o = 2 * a_width
        z = p(o + b_width, b_width) * p(o + 2 * b_width, b_width)
        _slab_store(zbuf, hb + r0, z)
        zb = (_slab_window(zbuf, r0 + hb - 2, rows) * wb_ref[0:1, :]
              + _slab_window(zbuf, r0 + hb - 1, rows) * wb_ref[1:2, :]
              + z * wb_ref[2:3, :])
        yb = p(o, b_width) * zb * beta_b_ref[...]
        mix_ref[pl.ds(r0, rows), a_width:a_width + b_width] = yb.astype(_BF16)

    upbufs = (upbuf0, upbuf1)

    def ffn_up(j, buf):
        _slab_store(buf, 0, ucar[j])
        _slab_store(buf, hb, _ff_up_chunk(h2_ref[...], wup_ref, j, d_ff))
        ucar[j] = _slab_load(buf, tile, hb)

    def ffn_act(j, buf):
        w = _ff_cols(wcf_ref, j, d_ff)
        for r0 in range(0, tile, ffn_rows):
            cv = (_slab_window(buf, r0 + hb - 2, ffn_rows) * w[0:1, :]
                  + _slab_window(buf, r0 + hb - 1, ffn_rows) * w[1:2, :]
                  + _slab_load(buf, r0 + hb, ffn_rows) * w[2:3, :])
            a = _silu(cv[:, 0:FF_CHUNK]) * cv[:, FF_CHUNK:2 * FF_CHUNK]
            act_ref[j, r0:r0 + ffn_rows, :] = a.astype(_BF16)

    def ffn_down(j, n):
        acc = x2_ref[...]
        for i in range(n):
            r0 = (j + i) * FF_CHUNK
            if not isinstance(j, int):
                r0 = pl.multiple_of(r0, FF_CHUNK)
            acc = acc + jnp.dot(act_ref[j + i], wdown_ref[pl.ds(r0, FF_CHUNK), :],
                                preferred_element_type=_F32)
        x2_ref[...] = acc

    n_mix = tile // mix_rows
    n_iter = (n_chunks - 3) // 2
    assert n_chunks >= 3 and mix_first + n_iter * mix_per_pair <= n_mix

    x2_ref[...] = x1_ref[...]
    ffn_up(0, upbuf0)
    ffn_up(1, upbuf1)
    ffn_act(0, upbuf0)
    for i in range(mix_first):
        mixer_chunk(i)

    def ffn_pair(k, c):
        j = 2 * k
        ffn_up(j + 2, upbuf0)
        ffn_act(j + 1, upbuf1)
        ffn_up(j + 3, upbuf1)
        ffn_act(j + 2, upbuf0)
        ffn_down(j, 2)
        for i in range(mix_per_pair):
            mixer_chunk(mix_first + k * mix_per_pair + i)
        return c
    lax.fori_loop(0, n_iter, ffn_pair, 0)
    j_done = 2 * n_iter
    for j in range(j_done + 2, n_chunks):
        ffn_up(j, upbufs[j % 2])
        ffn_act(j - 1, upbufs[(j - 1) % 2])
    ffn_act(n_chunks - 1, upbufs[(n_chunks - 1) % 2])
    ffn_down(j_done, n_chunks - j_done)

    for i in range(mix_first + n_iter * mix_per_pair, n_mix):
        mixer_chunk(i)
    x1_ref[...] = x_ref[...] + jnp.dot(mix_ref[...], wout_ref[...],
                                       preferred_element_type=_F32)
    norm_rows_static(x2_ref, gf_ref, y_ref, _F32)
    norm_rows_static(x1_ref, g2_ref, h2_ref, _BF16)
    _slab_store(ubuf, 0, _slab_load(ubuf, tile, ha))
    _slab_store(zbuf, 0, _slab_load(zbuf, tile, hb))

    @pl.when(jnp.logical_and(has_ffn, t_ffn == nt - 1))
    def _():
        if exact_states:
            for j in range(n_chunks):
                last = ucar[j, hb - 2:hb, :]
                for half in range(2):
                    c0 = half * d_ff + j * FF_CHUNK
                    cf_ref[:, c0:c0 + FF_CHUNK] = last[:, half * FF_CHUNK:(half + 1) * FF_CHUNK]
        else:
            cf_ref[...] = ucar[...]

    @pl.when(jnp.logical_and(has_mix, t_mix == nt - 1))
    def _():
        if exact_states:
            ca_ref[...] = _slab_window(ubuf, off_a, conv_a_w - 1)
            cb_ref[...] = _slab_window(zbuf, hb - 2, 2)
        else:
            ca_ref[...] = _slab_load(ubuf, 0, ha)
            cb_ref[...] = _slab_load(zbuf, 0, hb)


def _seq_layer(x, ca0, cb0, cf0, wts, *, tile, norm_rows, mix_rows, mix_first, mix_per_pair,
               ffn_rows, exact_states, name):
    nb, length, d = x.shape
    nt = length // tile
    n_tiles = nb * nt
    (g1, win, wa, ba, lng, lnb, wb, beta_a, beta_b, wout, g2, wup, wcf, wdown, gf) = wts
    a_width = wa.shape[1]
    b_width = wb.shape[1]
    conv_a_w = wa.shape[0]
    d_ff = wdown.shape[0]
    n_chunks = d_ff // FF_CHUNK
    ha, hb = ca0.shape[0], cb0.shape[0]
    kern = functools.partial(_seq_kernel, tile=tile, nt=nt, n_tiles=n_tiles,
                             norm_rows=norm_rows, mix_rows=mix_rows,
                             mix_first=mix_first, mix_per_pair=mix_per_pair, ffn_rows=ffn_rows,
                             a_width=a_width, b_width=b_width, conv_a_w=conv_a_w,
                             d_ff=d_ff, exact_states=exact_states)
    mix_tile = lambda s: jnp.minimum(s, n_tiles - 1)
    ffn_tile = lambda s: jnp.maximum(s - 1, 0)
    const_spec = lambda a: pl.BlockSpec(a.shape, lambda s: (0,) * a.ndim,
                                        pipeline_mode=pl.Buffered(1))
    consts = (ca0, cb0, cf0) + tuple(wts)
    in_specs = [pl.BlockSpec((None, tile, d),
                             lambda s: (mix_tile(s) // nt, mix_tile(s) % nt, 0))]
    in_specs += [const_spec(c) for c in consts]
    if exact_states:
        state_shapes = ((conv_a_w - 1, a_width), (2, b_width), (2, 2 * d_ff))
    else:
        state_shapes = (ca0.shape, cb0.shape, cf0.shape)
    state_tiles = (mix_tile, mix_tile, ffn_tile)
    out_shape = (jax.ShapeDtypeStruct((nb, length, d), _F32),) + tuple(
        jax.ShapeDtypeStruct((nb,) + shp, _F32) for shp in state_shapes)
    out_specs = (
        pl.BlockSpec((None, tile, d), lambda s: (ffn_tile(s) // nt, ffn_tile(s) % nt, 0)),
    ) + tuple(
        pl.BlockSpec((None,) + shp, lambda s, f=f, n=len(shp): (f(s) // nt,) + (0,) * n)
        for shp, f in zip(state_shapes, state_tiles))
    scratch = [
        pltpu.VMEM((a_width // LANES, ha + tile, LANES), _F32),
        pltpu.VMEM((b_width // LANES, hb + tile, LANES), _F32),
        pltpu.VMEM(cf0.shape, _F32),
        pltpu.VMEM((2 * FF_CHUNK // LANES, hb + tile, LANES), _F32),
        pltpu.VMEM((2 * FF_CHUNK // LANES, hb + tile, LANES), _F32),
        pltpu.VMEM((tile, d), _BF16),
        pltpu.VMEM((tile, d), _BF16),
        pltpu.VMEM((tile, win.shape[1]), _F32),
        pltpu.VMEM((tile, a_width + b_width), _BF16),
        pltpu.VMEM((tile, d), _F32),
        pltpu.VMEM((tile, d), _F32),
        pltpu.VMEM((n_chunks, tile, FF_CHUNK), _BF16),
    ]
    return pl.pallas_call(
        kern,
        grid=(n_tiles + 1,),
        in_specs=in_specs,
        out_specs=out_specs,
        out_shape=out_shape,
        scratch_shapes=scratch,
        compiler_params=pltpu.CompilerParams(
            dimension_semantics=("arbitrary",),
            vmem_limit_bytes=VMEM_LIMIT_BYTES),
        name=name,
    )(x, *consts)


def _sample_kernel(xs_ref, sa_ref, sb_ref, sf_ref, g1_ref, win_ref, wa_ref, ba_ref,
                   lng_ref, lnb_ref, wb_ref, beta_a_ref, beta_b_ref, wout_ref, g2_ref,
                   wup_ref, wcf_ref, wdown_ref, gf_ref,
                   ys_ref, na_ref, nb_ref, nf_ref,
                   h_ref, proj_ref, u_ref, z_ref, mix_ref, x1_ref, up_ref, act_ref,
                   *, steps, d, a_width, b_width, conv_a_w, d_ff, n_chunks):
    s = xs_ref.shape[0]
    hist_a = conv_a_w - 1
    rows_of = lambda t: slice(t * s, (t + 1) * s)

    for t in range(steps):
        h_ref[rows_of(t), :] = _rmsnorm(xs_ref[:, t, :], g1_ref[...]).astype(_BF16)
    proj_ref[...] = jnp.dot(h_ref[...], win_ref[...], preferred_element_type=_F32)

    o = 2 * a_width
    for t in range(steps):
        r = rows_of(t)
        u_ref[r, :] = proj_ref[r, 0:a_width] * _sigmoid(proj_ref[r, a_width:o])
        z_ref[r, :] = (proj_ref[r, o + b_width:o + 2 * b_width]
                       * proj_ref[r, o + 2 * b_width:o + 3 * b_width])

    def xe_a(j):
        return sa_ref[:, j, :] if j < hist_a else u_ref[rows_of(j - hist_a), :]

    def xe_b(j):
        return sb_ref[:, j, :] if j < 2 else z_ref[rows_of(j - 2), :]

    for t in range(steps):
        r = rows_of(t)
        acc = jnp.broadcast_to(ba_ref[...], (s, a_width))
        for k in range(conv_a_w):
            acc = acc + xe_a(t + k) * wa_ref[k:k + 1, :]
        ya = _group_a_out(acc, lng_ref[...], lnb_ref[...], beta_a_ref[...])
        mix_ref[r, 0:a_width] = ya.astype(_BF16)
        zb = (xe_b(t) * wb_ref[0:1, :] + xe_b(t + 1) * wb_ref[1:2, :]
              + xe_b(t + 2) * wb_ref[2:3, :])
        yb = proj_ref[r, o:o + b_width] * zb * beta_b_ref[...]
        mix_ref[r, a_width:a_width + b_width] = yb.astype(_BF16)

    for j in range(hist_a):
        na_ref[:, j, :] = xe_a(j + steps)
    for j in range(2):
        nb_ref[:, j, :] = xe_b(j + steps)

    x1_ref[...] = jnp.dot(mix_ref[...], wout_ref[...], preferred_element_type=_F32)
    for t in range(steps):
        r = rows_of(t)
        x1 = x1_ref[r, :] + xs_ref[:, t, :]
        x1_ref[r, :] = x1
        h_ref[r, :] = _rmsnorm(x1, g2_ref[...]).astype(_BF16)

    for j in range(n_chunks):
        up_ref[...] = _ff_up_chunk(h_ref[...], wup_ref, j, d_ff)
        w = _ff_cols(wcf_ref, j, d_ff)
        lo = j * FF_CHUNK

        def xe_f(i, half):
            if i < 2:
                c0 = half * d_ff + lo
                return sf_ref[:, i, c0:c0 + FF_CHUNK]
            return up_ref[rows_of(i - 2), half * FF_CHUNK:(half + 1) * FF_CHUNK]

        for t in range(steps):
            gv = []
            for half in range(2):
                wh = w[:, half * FF_CHUNK:(half + 1) * FF_CHUNK]
                gv.append(xe_f(t, half) * wh[0:1, :] + xe_f(t + 1, half) * wh[1:2, :]
                          + xe_f(t + 2, half) * wh[2:3, :])
            act_ref[j, rows_of(t), :] = (_silu(gv[0]) * gv[1]).astype(_BF16)
        for i in range(2):
            for half in range(2):
                c0 = half * d_ff + lo
                nf_ref[:, i, c0:c0 + FF_CHUNK] = xe_f(i + steps, half)

    acc = _ff_down_chain(x1_ref[...], act_ref, wdown_ref, n_chunks)
    for t in range(steps):
        ys_ref[:, t, :] = _rmsnorm(acc[rows_of(t), :], gf_ref[...])


def _sample_layer(xs, sa, sb, sf, wts, *, steps, seq_block):
    n = xs.shape[0]
    (g1, win, wa, ba, lng, lnb, wb, beta_a, beta_b, wout, g2, wup, wcf, wdown, gf) = wts
    d = g1.shape[1]
    a_width = wa.shape[1]
    b_width = wb.shape[1]
    conv_a_w = wa.shape[0]
    d_ff = wdown.shape[0]
    n_chunks = d_ff // FF_CHUNK
    m = steps * seq_block
    kern = functools.partial(_sample_kernel, steps=steps, d=d, a_width=a_width,
                             b_width=b_width, conv_a_w=conv_a_w, d_ff=d_ff,
                             n_chunks=n_chunks)
    row_spec = lambda a: pl.BlockSpec((seq_block,) + a.shape[1:], lambda i: (i, 0, 0))
    const_spec = lambda a: pl.BlockSpec(a.shape, lambda i: (0,) * a.ndim,
                                        pipeline_mode=pl.Buffered(1))
    data = (xs, sa, sb, sf)
    in_specs = [row_spec(a) for a in data] + [const_spec(a) for a in wts]
    out_shape = tuple(jax.ShapeDtypeStruct(a.shape, _F32) for a in data)
    out_specs = tuple(row_spec(a) for a in data)
    scratch = [
        pltpu.VMEM((m, d), _BF16),
        pltpu.VMEM((m, win.shape[1]), _F32),
        pltpu.VMEM((m, a_width), _F32),
        pltpu.VMEM((m, b_width), _F32),
        pltpu.VMEM((m, a_width + b_width), _BF16),
        pltpu.VMEM((m, d), _F32),
        pltpu.VMEM((m, 2 * FF_CHUNK), _F32),
        pltpu.VMEM((n_chunks, m, FF_CHUNK), _BF16),
    ]
    return pl.pallas_call(
        kern,
        grid=(n // seq_block,),
        in_specs=in_specs,
        out_specs=out_specs,
        out_shape=out_shape,
        scratch_shapes=scratch,
        compiler_params=pltpu.CompilerParams(
            dimension_semantics=("arbitrary",),
            vmem_limit_bytes=VMEM_LIMIT_BYTES),
        name="sample_layer",
    )(*data, *wts)


def kernel(x_prompt, x_sample, state_conv_a, state_conv_b, state_conv_ffn, meta_tokens,
           norm_mix_g, w_in, w_conv_a, b_conv_a, gn_a_g, gn_a_b, w_conv_b, beta_a, beta_b,
           w_out, norm_ffn_g, w_up, w_conv_f, w_down, norm_final_g):
    depth = w_in.shape[0]
    assert depth == 1, "single-layer step only"
    batch, seq, d = x_prompt.shape
    n_dec, dec_seq, _ = x_sample.shape
    n_meta = meta_tokens.shape[0]
    conv_a_w, a_width = w_conv_a.shape[1:]
    b_width = w_conv_b.shape[2]
    d_ff = w_down.shape[1]
    assert a_width % LANES == 0 and LANES == 2 * HEAD_DIM
    assert d_ff % FF_CHUNK == 0 and w_conv_b.shape[1] == 3 and w_conv_f.shape[1] == 3
    n_chunks = d_ff // FF_CHUNK
    ha = -(-(conv_a_w - 1) // SUBLANES) * SUBLANES
    hb = SUBLANES

    row = lambda v: v.reshape(1, -1)
    wts = (
        row(norm_mix_g[0]),
        w_in[0].astype(_BF16),
        w_conv_a[0], row(b_conv_a[0]), row(gn_a_g[0]), row(gn_a_b[0]),
        w_conv_b[0], row(beta_a[0]), row(beta_b[0]),
        w_out[0].astype(_BF16),
        row(norm_ffn_g[0]),
        w_up[0].astype(_BF16),
        w_conv_f[0],
        w_down[0].astype(_BF16),
        row(norm_final_g),
    )

    zeros = lambda *s: jnp.zeros(s, _F32)
    _, ca_m, cb_m, cf_m = _seq_layer(
        meta_tokens[None], zeros(ha, a_width), zeros(hb, b_width),
        zeros(n_chunks, hb, 2 * FF_CHUNK), wts, tile=n_meta, norm_rows=n_meta, mix_rows=n_meta,
        mix_first=0, mix_per_pair=0, ffn_rows=n_meta, exact_states=False, name="meta_layer")

    y_prompt, na_p, nb_p, nf_p = _seq_layer(
        x_prompt, ca_m[0], cb_m[0], cf_m[0], wts, tile=512, norm_rows=64, mix_rows=32,
        mix_first=3, mix_per_pair=2, ffn_rows=32, exact_states=True, name="prompt_layer")

    y_s, na_s, nb_s, nf_s = _sample_layer(
        x_sample, state_conv_a[0], state_conv_b[0], state_conv_ffn[0],
        wts, steps=dec_seq, seq_block=32)

    return (y_prompt, y_s, na_p[None], nb_p[None], nf_p[None], na_s[None], nb_s[None],
            nf_s[None])
```

```python
import functools

import jax
import jax.numpy as jnp
from jax import lax
from jax.experimental import pallas as pl
from jax.experimental.pallas import tpu as pltpu

EPS = 1e-6
HEAD_DIM = 64
LANES = 128
SUBLANES = 8
FF_CHUNK = 256
VMEM_LIMIT_BYTES = 56 * 1024 * 1024

_F32 = jnp.float32
_BF16 = jnp.bfloat16


def _rmsnorm(x, g):
    ms = jnp.mean(x * x, axis=-1, keepdims=True)
    return x * lax.rsqrt(ms + EPS) * g


def _sigmoid(x):
    return 1.0 / (1.0 + jnp.exp(-x))


def _silu(x):
    return x * _sigmoid(x)


def _group_mean(v, lane_lo):
    sa = jnp.sum(jnp.where(lane_lo, v, 0.0), axis=1, keepdims=True)
    sb = jnp.sum(jnp.where(lane_lo, 0.0, v), axis=1, keepdims=True)
    return jnp.where(lane_lo, sa, sb) * (1.0 / HEAD_DIM)


def _head_layernorm(ua):
    rows, width = ua.shape
    lane_lo = lax.broadcasted_iota(jnp.int32, (rows, LANES), 1) < HEAD_DIM
    outs = []
    for j in range(width // LANES):
        xb = ua[:, j * LANES:(j + 1) * LANES]
        d = xb - _group_mean(xb, lane_lo)
        var = _group_mean(d * d, lane_lo)
        outs.append(d * lax.rsqrt(var + EPS))
    return jnp.concatenate(outs, axis=1)


def _group_a_out(ua, lng, lnb, beta_a):
    return _silu(_head_layernorm(ua) * lng + lnb) * beta_a


def _ff_cols(ref, j, d_ff):
    off = j * FF_CHUNK
    if not isinstance(j, int):
        off = pl.multiple_of(off, FF_CHUNK)
    return jnp.concatenate([ref[:, pl.ds(off, FF_CHUNK)],
                            ref[:, pl.ds(d_ff + off, FF_CHUNK)]], axis=1)


def _ff_up_chunk(h, wup_ref, j, d_ff):
    off = j * FF_CHUNK
    if not isinstance(j, int):
        off = pl.multiple_of(off, FF_CHUNK)
    g = jnp.dot(h, wup_ref[:, pl.ds(off, FF_CHUNK)], preferred_element_type=_F32)
    v = jnp.dot(h, wup_ref[:, pl.ds(d_ff + off, FF_CHUNK)], preferred_element_type=_F32)
    return jnp.concatenate([g, v], axis=1)


def _ff_down_chain(acc, act_ref, wdown_ref, n_chunks):
    for j in range(n_chunks):
        acc = acc + jnp.dot(act_ref[j], wdown_ref[j * FF_CHUNK:(j + 1) * FF_CHUNK, :],
                            preferred_element_type=_F32)
    return acc


def _slab_store(buf, row0, val):
    for c in range(buf.shape[0]):
        buf[c, pl.ds(row0, val.shape[0]), :] = val[:, c * LANES:(c + 1) * LANES]


def _slab_load(buf, row0, nrows):
    return jnp.concatenate(
        [buf[c, pl.ds(row0, nrows), :] for c in range(buf.shape[0])], axis=1)


def _slab_window(buf, row0, nrows):
    return jnp.concatenate(
        [buf[c, pl.ds(row0, nrows, stride=1), :] for c in range(buf.shape[0])], axis=1)


def _seq_kernel(x_ref, ca0_ref, cb0_ref, cf0_ref, g1_ref, win_ref, wa_ref, ba_ref,
                lng_ref, lnb_ref, wb_ref, beta_a_ref, beta_b_ref, wout_ref, g2_ref,
                wup_ref, wcf_ref, wdown_ref, gf_ref,
                y_ref, ca_ref, cb_ref, cf_ref,
                ubuf, zbuf, ucar, upbuf0, upbuf1, h1_ref, h2_ref, proj_ref, mix_ref, x1_ref,
                x2_ref, act_ref,
                *, tile, nt, n_tiles, norm_rows, mix_rows, mix_first, mix_per_pair, ffn_rows,
                a_width, b_width, conv_a_w, d_ff, exact_states):
    n_chunks = d_ff // FF_CHUNK
    s = pl.program_id(0)
    t_mix = lax.rem(s, nt)
    t_ffn = lax.rem(s + nt - 1, nt)
    has_mix = s < n_tiles
    has_ffn = s > 0
    ha = ubuf.shape[1] - tile
    hb = zbuf.shape[1] - tile
    off_a = ha - (conv_a_w - 1)

    @pl.when(s == 0)
    def _():
        h2_ref[...] = jnp.zeros(h2_ref.shape, h2_ref.dtype)
        x1_ref[...] = jnp.zeros(x1_ref.shape, x1_ref.dtype)

    @pl.when(t_mix == 0)
    def _():
        _slab_store(ubuf, 0, ca0_ref[...])
        _slab_store(zbuf, 0, cb0_ref[...])

    @pl.when(jnp.logical_or(s == 0, t_ffn == 0))
    def _():
        ucar[...] = cf0_ref[...]

    def norm_rows_static(src_ref, g_ref, dst_ref, dtype):
        for r0 in range(0, tile, norm_rows):
            x = src_ref[r0:r0 + norm_rows, :]
            dst_ref[r0:r0 + norm_rows, :] = _rmsnorm(x, g_ref[...]).astype(dtype)

    @pl.when(has_mix)
    def _():
        norm_rows_static(x_ref, g1_ref, h1_ref, _BF16)
        proj_ref[...] = jnp.dot(h1_ref[...], win_ref[...], preferred_element_type=_F32)

    def mixer_chunk(i):
        rows = mix_rows
        r0 = i * rows if isinstance(i, int) else pl.multiple_of(i * rows, rows)
        p = lambda lo, w: proj_ref[pl.ds(r0, rows), lo:lo + w]
        u = p(0, a_width) * _sigmoid(p(a_width, a_width))
        _slab_store(ubuf, ha + r0, u)
        acc = jnp.broadcast_to(ba_ref[...], (rows, a_width))
        for k in range(conv_a_w):
            acc = acc + _slab_window(ubuf, r0 + off_a + k, rows) * wa_ref[k:k + 1, :]
        ya = _group_a_out(acc, lng_ref[...], lnb_ref[...], beta_a_ref[...])
        mix_ref[pl.ds(r0, rows), 0:a_width] = ya.astype(_BF16)

        o = 2 * a_width
        z = p(o + b_width, b_width) * p(o + 2 * b_width, b_width)
        _slab_store(zbuf, hb + r0, z)
        zb = (_slab_window(zbuf, r0 + hb - 2, rows) * wb_ref[0:1, :]
              + _slab_window(zbuf, r0 + hb - 1, rows) * wb_ref[1:2, :]
              + z * wb_ref[2:3, :])
        yb = p(o, b_width) * zb * beta_b_ref[...]
        mix_ref[pl.ds(r0, rows), a_width:a_width + b_width] = yb.astype(_BF16)

    upbufs = (upbuf0, upbuf1)

    def ffn_up(j, buf):
        _slab_store(buf, 0, ucar[j])
        _slab_store(buf, hb, _ff_up_chunk(h2_ref[...], wup_ref, j, d_ff))
        ucar[j] = _slab_load(buf, tile, hb)

    def ffn_act(j, buf):
        w = _ff_cols(wcf_ref, j, d_ff)
        for r0 in range(0, tile, ffn_rows):
            cv = (_slab_window(buf, r0 + hb - 2, ffn_rows) * w[0:1, :]
                  + _slab_window(buf, r0 + hb - 1, ffn_rows) * w[1:2, :]
                  + _slab_load(buf, r0 + hb, ffn_rows) * w[2:3, :])
            a = _silu(cv[:, 0:FF_CHUNK]) * cv[:, FF_CHUNK:2 * FF_CHUNK]
            act_ref[j, r0:r0 + ffn_rows, :] = a.astype(_BF16)

    def ffn_down(j, n):
        acc = x2_ref[...]
        for i in range(n):
            r0 = (j + i) * FF_CHUNK
            if not isinstance(j, int):
                r0 = pl.multiple_of(r0, FF_CHUNK)
            acc = acc + jnp.dot(act_ref[j + i], wdown_ref[pl.ds(r0, FF_CHUNK), :],
                                preferred_element_type=_F32)
        x2_ref[...] = acc

    n_mix = tile // mix_rows
    n_iter = (n_chunks - 3) // 2
    assert n_chunks >= 3 and mix_first + n_iter * mix_per_pair <= n_mix

    x2_ref[...] = x1_ref[...]
    ffn_up(0, upbuf0)
    ffn_up(1, upbuf1)
    ffn_act(0, upbuf0)
    for i in range(mix_first):
        mixer_chunk(i)

    def ffn_pair(k, c):
        j = 2 * k
        ffn_up(j + 2, upbuf0)
        ffn_act(j + 1, upbuf1)
        ffn_up(j + 3, upbuf1)
        ffn_act(j + 2, upbuf0)
        ffn_down(j, 2)
        for i in range(mix_per_pair):
            mixer_chunk(mix_first + k * mix_per_pair + i)
        return c
    lax.fori_loop(0, n_iter, ffn_pair, 0)
    j_done = 2 * n_iter
    for j in range(j_done + 2, n_chunks):
        ffn_up(j, upbufs[j % 2])
        ffn_act(j - 1, upbufs[(j - 1) % 2])
    ffn_act(n_chunks - 1, upbufs[(n_chunks - 1) % 2])
    ffn_down(j_done, n_chunks - j_done)

    for i in range(mix_first + n_iter * mix_per_pair, n_mix):
        mixer_chunk(i)
    x1_ref[...] = x_ref[...] + jnp.dot(mix_ref[...], wout_ref[...],
                                       preferred_element_type=_F32)
    norm_rows_static(x2_ref, gf_ref, y_ref, _F32)
    norm_rows_static(x1_ref, g2_ref, h2_ref, _BF16)
    _slab_store(ubuf, 0, _slab_load(ubuf, tile, ha))
    _slab_store(zbuf, 0, _slab_load(zbuf, tile, hb))

    @pl.when(jnp.logical_and(has_ffn, t_ffn == nt - 1))
    def _():
        if exact_states:
            for j in range(n_chunks):
                last = ucar[j, hb - 2:hb, :]
                for half in range(2):
                    c0 = half * d_ff + j * FF_CHUNK
                    cf_ref[:, c0:c0 + FF_CHUNK] = last[:, half * FF_CHUNK:(half + 1) * FF_CHUNK]
        else:
            cf_ref[...] = ucar[...]

    @pl.when(jnp.logical_and(has_mix, t_mix == nt - 1))
    def _():
        if exact_states:
            ca_ref[...] = _slab_window(ubuf, off_a, conv_a_w - 1)
            cb_ref[...] = _slab_window(zbuf, hb - 2, 2)
        else:
            ca_ref[...] = _slab_load(ubuf, 0, ha)
            cb_ref[...] = _slab_load(zbuf, 0, hb)


def _seq_layer(x, ca0, cb0, cf0, wts, *, tile, norm_rows, mix_rows, mix_first, mix_per_pair,
               ffn_rows, exact_states, name):
    nb, length, d = x.shape
    nt = length // tile
    n_tiles = nb * nt
    (g1, win, wa, ba, lng, lnb, wb, beta_a, beta_b, wout, g2, wup, wcf, wdown, gf) = wts
    a_width = wa.shape[1]
    b_width = wb.shape[1]
    conv_a_w = wa.shape[0]
    d_ff = wdown.shape[0]
    n_chunks = d_ff // FF_CHUNK
    ha, hb = ca0.shape[0], cb0.shape[0]
    kern = functools.partial(_seq_kernel, tile=tile, nt=nt, n_tiles=n_tiles,
                             norm_rows=norm_rows, mix_rows=mix_rows,
                             mix_first=mix_first, mix_per_pair=mix_per_pair, ffn_rows=ffn_rows,
                             a_width=a_width, b_width=b_width, conv_a_w=conv_a_w,
                             d_ff=d_ff, exact_states=exact_states)
    mix_tile = lambda s: jnp.minimum(s, n_tiles - 1)
    ffn_tile = lambda s: jnp.maximum(s - 1, 0)
    const_spec = lambda a: pl.BlockSpec(a.shape, lambda s: (0,) * a.ndim,
                                        pipeline_mode=pl.Buffered(1))
    consts = (ca0, cb0, cf0) + tuple(wts)
    in_specs = [pl.BlockSpec((None, tile, d),
                             lambda s: (mix_tile(s) // nt, mix_tile(s) % nt, 0))]
    in_specs += [const_spec(c) for c in consts]
    if exact_states:
        state_shapes = ((conv_a_w - 1, a_width), (2, b_width), (2, 2 * d_ff))
    else:
        state_shapes = (ca0.shape, cb0.shape, cf0.shape)
    state_tiles = (mix_tile, mix_tile, ffn_tile)
    out_shape = (jax.ShapeDtypeStruct((nb, length, d), _F32),) + tuple(
        jax.ShapeDtypeStruct((nb,) + shp, _F32) for shp in state_shapes)
    out_specs = (
        pl.BlockSpec((None, tile, d), lambda s: (ffn_tile(s) // nt, ffn_tile(s) % nt, 0)),
    ) + tuple(
        pl.BlockSpec((None,) + shp, lambda s, f=f, n=len(shp): (f(s) // nt,) + (0,) * n)
        for shp, f in zip(state_shapes, state_tiles))
    scratch = [
        pltpu.VMEM((a_width // LANES, ha + tile, LANES), _F32),
        pltpu.VMEM((b_width // LANES, hb + tile, LANES), _F32),
        pltpu.VMEM(cf0.shape, _F32),
        pltpu.VMEM((2 * FF_CHUNK // LANES, hb + tile, LANES), _F32),
        pltpu.VMEM((2 * FF_CHUNK // LANES, hb + tile, LANES), _F32),
        pltpu.VMEM((tile, d), _BF16),
        pltpu.VMEM((tile, d), _BF16),
        pltpu.VMEM((tile, win.shape[1]), _F32),
        pltpu.VMEM((tile, a_width + b_width), _BF16),
        pltpu.VMEM((tile, d), _F32),
        pltpu.VMEM((tile, d), _F32),
        pltpu.VMEM((n_chunks, tile, FF_CHUNK), _BF16),
    ]
    return pl.pallas_call(
        kern,
        grid=(n_tiles + 1,),
        in_specs=in_specs,
        out_specs=out_specs,
        out_shape=out_shape,
        scratch_shapes=scratch,
        compiler_params=pltpu.CompilerParams(
            dimension_semantics=("arbitrary",),
            vmem_limit_bytes=VMEM_LIMIT_BYTES),
        name=name,
    )(x, *consts)


def _sample_kernel(xs_ref, sa_ref, sb_ref, sf_ref, g1_ref, win_ref, wa_ref, ba_ref,
                   lng_ref, lnb_ref, wb_ref, beta_a_ref, beta_b_ref, wout_ref, g2_ref,
                   wup_ref, wcf_ref, wdown_ref, gf_ref,
                   ys_ref, na_ref, nb_ref, nf_ref,
                   h_ref, proj_ref, u_ref, z_ref, mix_ref, x1_ref, up_ref, act_ref, hist_ref,
                   pa_ref, *, steps, d, a_width, b_width, conv_a_w, d_ff, n_chunks):
    s = xs_ref.shape[0]
    hist_a = conv_a_w - 1
    rows_of = lambda t: slice(t * s, (t + 1) * s)

    for t in range(steps):
        h_ref[rows_of(t), :] = _rmsnorm(xs_ref[:, t, :], g1_ref[...]).astype(_BF16)
    proj_ref[...] = jnp.dot(h_ref[...], win_ref[...], preferred_element_type=_F32)

    o = 2 * a_width
    for t in range(steps):
        r = rows_of(t)
        u_ref[r, :] = proj_ref[r, 0:a_width] * _sigmoid(proj_ref[r, a_width:o])
        z_ref[r, :] = (proj_ref[r, o + b_width:o + 2 * b_width]
                       * proj_ref[r, o + 2 * b_width:o + 3 * b_width])

    n_slabs = hist_ref.shape[0]
    pad_rows = hist_ref.shape[2] - hist_a
    for c in range(n_slabs):
        hist_ref[c, :, 0:hist_a, :] = sa_ref[:, :, c * LANES:(c + 1) * LANES]
        hist_ref[c, :, hist_a:hist_a + pad_rows, :] = jnp.zeros((s, pad_rows, LANES), _F32)
    for s0 in range(0, s, SUBLANES):
        parts = []
        for c in range(n_slabs):
            acc = jnp.zeros((SUBLANES, steps, LANES), _F32)
            for k in range(hist_a):
                win = hist_ref[c, s0:s0 + SUBLANES, pl.ds(k, steps, stride=1), :]
                acc = acc + win * wa_ref[k:k + 1, c * LANES:(c + 1) * LANES]
            parts.append(acc)
        pa_ref[s0:s0 + SUBLANES, :, :] = jnp.concatenate(parts, axis=2)

    def xe_b(j):
        return sb_ref[:, j, :] if j < 2 else z_ref[rows_of(j - 2), :]

    for t in range(steps):
        r = rows_of(t)
        acc = ba_ref[...] + pa_ref[:, t, :]
        for k in range(hist_a - t, conv_a_w):
            acc = acc + u_ref[rows_of(t + k - hist_a), :] * wa_ref[k:k + 1, :]
        ya = _group_a_out(acc, lng_ref[...], lnb_ref[...], beta_a_ref[...])
        mix_ref[r, 0:a_width] = ya.astype(_BF16)
        zb = (xe_b(t) * wb_ref[0:1, :] + xe_b(t + 1) * wb_ref[1:2, :]
              + xe_b(t + 2) * wb_ref[2:3, :])
        yb = proj_ref[r, o:o + b_width] * zb * beta_b_ref[...]
        mix_ref[r, a_width:a_width + b_width] = yb.astype(_BF16)

    na_ref[:, 0:hist_a - steps, :] = sa_ref[:, steps:hist_a, :]
    for t in range(steps):
        na_ref[:, hist_a - steps + t, :] = u_ref[rows_of(t), :]
    for j in range(2):
        nb_ref[:, j, :] = xe_b(j + steps)

    x1_ref[...] = jnp.dot(mix_ref[...], wout_ref[...], preferred_element_type=_F32)
    for t in range(steps):
        r = rows_of(t)
        x1 = x1_ref[r, :] + xs_ref[:, t, :]
        x1_ref[r, :] = x1
        h_ref[r, :] = _rmsnorm(x1, g2_ref[...]).astype(_BF16)

    for j in range(n_chunks):
        up_ref[...] = _ff_up_chunk(h_ref[...], wup_ref, j, d_ff)
        w = _ff_cols(wcf_ref, j, d_ff)
        lo = j * FF_CHUNK

        def xe_f(i, half):
            if i < 2:
                c0 = half * d_ff + lo
                return sf_ref[:, i, c0:c0 + FF_CHUNK]
            return up_ref[rows_of(i - 2), half * FF_CHUNK:(half + 1) * FF_CHUNK]

        for t in range(steps):
            gv = []
            for half in range(2):
                wh = w[:, half * FF_CHUNK:(half + 1) * FF_CHUNK]
                gv.append(xe_f(t, half) * wh[0:1, :] + xe_f(t + 1, half) * wh[1:2, :]
                          + xe_f(t + 2, half) * wh[2:3, :])
            act_ref[j, rows_of(t), :] = (_silu(gv[0]) * gv[1]).astype(_BF16)
        for i in range(2):
            for half in range(2):
                c0 = half * d_ff + lo
                nf_ref[:, i, c0:c0 + FF_CHUNK] = xe_f(i + steps, half)

    acc = _ff_down_chain(x1_ref[...], act_ref, wdown_ref, n_chunks)
    for t in range(steps):
        ys_ref[:, t, :] = _rmsnorm(acc[rows_of(t), :], gf_ref[...])


def _sample_layer(xs, sa, sb, sf, wts, *, steps, seq_block):
    n = xs.shape[0]
    (g1, win, wa, ba, lng, lnb, wb, beta_a, beta_b, wout, g2, wup, wcf, wdown, gf) = wts
    d = g1.shape[1]
    a_width = wa.shape[1]
    b_width = wb.shape[1]
    conv_a_w = wa.shape[0]
    d_ff = wdown.shape[0]
    n_chunks = d_ff // FF_CHUNK
    m = steps * seq_block
    kern = functools.partial(_sample_kernel, steps=steps, d=d, a_width=a_width,
                             b_width=b_width, conv_a_w=conv_a_w, d_ff=d_ff,
                             n_chunks=n_chunks)
    row_spec = lambda a: pl.BlockSpec((seq_block,) + a.shape[1:], lambda i: (i, 0, 0))
    const_spec = lambda a: pl.BlockSpec(a.shape, lambda i: (0,) * a.ndim,
                                        pipeline_mode=pl.Buffered(1))
    data = (xs, sa, sb, sf)
    in_specs = [row_spec(a) for a in data] + [const_spec(a) for a in wts]
    out_shape = tuple(jax.ShapeDtypeStruct(a.shape, _F32) for a in data)
    out_specs = tuple(row_spec(a) for a in data)
    scratch = [
        pltpu.VMEM((m, d), _BF16),
        pltpu.VMEM((m, win.shape[1]), _F32),
        pltpu.VMEM((m, a_width), _F32),
        pltpu.VMEM((m, b_width), _F32),
        pltpu.VMEM((m, a_width + b_width), _BF16),
        pltpu.VMEM((m, d), _F32),
        pltpu.VMEM((m, 2 * FF_CHUNK), _F32),
        pltpu.VMEM((n_chunks, m, FF_CHUNK), _BF16),
        pltpu.VMEM((a_width // LANES, seq_block, conv_a_w - 1 + steps + 2, LANES), _F32),
        pltpu.VMEM((seq_block, steps, a_width), _F32),
    ]
    return pl.pallas_call(
        kern,
        grid=(n // seq_block,),
        in_specs=in_specs,
        out_specs=out_specs,
        out_shape=out_shape,
        scratch_shapes=scratch,
        compiler_params=pltpu.CompilerParams(
            dimension_semantics=("arbitrary",),
            vmem_limit_bytes=VMEM_LIMIT_BYTES),
        name="sample_layer",
    )(*data, *wts)


def kernel(x_prompt, x_sample, state_conv_a, state_conv_b, state_conv_ffn, meta_tokens,
           norm_mix_g, w_in, w_conv_a, b_conv_a, gn_a_g, gn_a_b, w_conv_b, beta_a, beta_b,
           w_out, norm_ffn_g, w_up, w_conv_f, w_down, norm_final_g):
    depth = w_in.shape[0]
    assert depth == 1, "single-layer step only"
    batch, seq, d = x_prompt.shape
    n_dec, dec_seq, _ = x_sample.shape
    n_meta = meta_tokens.shape[0]
    conv_a_w, a_width = w_conv_a.shape[1:]
    b_width = w_conv_b.shape[2]
    d_ff = w_down.shape[1]
    assert a_width % LANES == 0 and LANES == 2 * HEAD_DIM
    assert d_ff % FF_CHUNK == 0 and w_conv_b.shape[1] == 3 and w_conv_f.shape[1] == 3
    n_chunks = d_ff // FF_CHUNK
    ha = -(-(conv_a_w - 1) // SUBLANES) * SUBLANES
    hb = SUBLANES

    row = lambda v: v.reshape(1, -1)
    wts = (
        row(norm_mix_g[0]),
        w_in[0].astype(_BF16),
        w_conv_a[0], row(b_conv_a[0]), row(gn_a_g[0]), row(gn_a_b[0]),
        w_conv_b[0], row(beta_a[0]), row(beta_b[0]),
        w_out[0].astype(_BF16),
        row(norm_ffn_g[0]),
        w_up[0].astype(_BF16),
        w_conv_f[0],
        w_down[0].astype(_BF16),
        row(norm_final_g),
    )

    zeros = lambda *s: jnp.zeros(s, _F32)
    _, ca_m, cb_m, cf_m = _seq_layer(
        meta_tokens[None], zeros(ha, a_width), zeros(hb, b_width),
        zeros(n_chunks, hb, 2 * FF_CHUNK), wts, tile=n_meta, norm_rows=n_meta, mix_rows=n_meta,
        mix_first=0, mix_per_pair=0, ffn_rows=n_meta, exact_states=False, name="meta_layer")

    y_prompt, na_p, nb_p, nf_p = _seq_layer(
        x_prompt, ca_m[0], cb_m[0], cf_m[0], wts, tile=512, norm_rows=64, mix_rows=32,
        mix_first=3, mix_per_pair=2, ffn_rows=32, exact_states=True, name="prompt_layer")

    y_s, na_s, nb_s, nf_s = _sample_layer(
        x_sample, state_conv_a[0], state_conv_b[0], state_conv_ffn[0],
        wts, steps=dec_seq, seq_block=32)

    return (y_prompt, y_s, na_p[None], nb_p[None], nf_p[None], na_s[None], nb_s[None],
            nf_s[None])
```

```python
import functools

import jax
import jax.numpy as jnp
from jax import lax
from jax.experimental import pallas as pl
from jax.experimental.pallas import tpu as pltpu

EPS = 1e-6
HEAD_DIM = 64
LANES = 128
SUBLANES = 8
FF_CHUNK = 256
VMEM_LIMIT_BYTES = 56 * 1024 * 1024

_F32 = jnp.float32
_BF16 = jnp.bfloat16


def _rmsnorm(x, g):
    ms = jnp.mean(x * x, axis=-1, keepdims=True)
    return x * lax.rsqrt(ms + EPS) * g


def _sigmoid(x):
    return 1.0 / (1.0 + jnp.exp(-x))


def _silu(x):
    return x * _sigmoid(x)


def _group_mean(v, lane_lo):
    sa = jnp.sum(jnp.where(lane_lo, v, 0.0), axis=1, keepdims=True)
    sb = jnp.sum(jnp.where(lane_lo, 0.0, v), axis=1, keepdims=True)
    return jnp.where(lane_lo, sa, sb) * (1.0 / HEAD_DIM)


def _head_layernorm(ua):
    rows, width = ua.shape
    lane_lo = lax.broadcasted_iota(jnp.int32, (rows, LANES), 1) < HEAD_DIM
    outs = []
    for j in range(width // LANES):
        xb = ua[:, j * LANES:(j + 1) * LANES]
        d = xb - _group_mean(xb, lane_lo)
        var = _group_mean(d * d, lane_lo)
        outs.append(d * lax.rsqrt(var + EPS))
    return jnp.concatenate(outs, axis=1)


def _group_a_out(ua, lng, lnb, beta_a):
    return _silu(_head_layernorm(ua) * lng + lnb) * beta_a


def _ff_cols(ref, j, d_ff):
    off = j * FF_CHUNK
    if not isinstance(j, int):
        off = pl.multiple_of(off, FF_CHUNK)
    return jnp.concatenate([ref[:, pl.ds(off, FF_CHUNK)],
                            ref[:, pl.ds(d_ff + off, FF_CHUNK)]], axis=1)


def _ff_up_chunk(h, wup_ref, j, d_ff):
    off = j * FF_CHUNK
    if not isinstance(j, int):
        off = pl.multiple_of(off, FF_CHUNK)
    g = jnp.dot(h, wup_ref[:, pl.ds(off, FF_CHUNK)], preferred_element_type=_F32)
    v = jnp.dot(h, wup_ref[:, pl.ds(d_ff + off, FF_CHUNK)], preferred_element_type=_F32)
    return jnp.concatenate([g, v], axis=1)


def _ff_down_chain(acc, act_ref, wdown_ref, n_chunks):
    for j in range(n_chunks):
        acc = acc + jnp.dot(act_ref[j], wdown_ref[j * FF_CHUNK:(j + 1) * FF_CHUNK, :],
                            preferred_element_type=_F32)
    return acc


def _slab_store(buf, row0, val):
    for c in range(buf.shape[0]):
        buf[c, pl.ds(row0, val.shape[0]), :] = val[:, c * LANES:(c + 1) * LANES]


def _slab_load(buf, row0, nrows):
    return jnp.concatenate(
        [buf[c, pl.ds(row0, nrows), :] for c in range(buf.shape[0])], axis=1)


def _slab_window(buf, row0, nrows):
    return jnp.concatenate(
        [buf[c, pl.ds(row0, nrows, stride=1), :] for c in range(buf.shape[0])], axis=1)


def _seq_kernel(x_ref, meta_ref, g1_ref, win_ref, wa_ref, ba_ref,
                lng_ref, lnb_ref, wb_ref, beta_a_ref, beta_b_ref, wout_ref, g2_ref,
                wup_ref, wcf_ref, wdown_ref, gf_ref,
                y_ref, ca_ref, cb_ref, cf_ref,
                ubuf, zbuf, ucar, ca0_ref, cb0_ref, cf0_ref, upbuf0, upbuf1, h1_ref, h2_ref,
                proj_ref, mix_ref, x1_ref, x2_ref, act_ref,
                *, tile, nt, n_tiles, norm_rows, mix_rows, mix_first, mix_per_pair, ffn_rows,
                a_width, b_width, conv_a_w, d_ff):
    n_chunks = d_ff // FF_CHUNK
    s = pl.program_id(0)
    t_mix = lax.rem(s, nt)
    t_ffn = lax.rem(s + nt - 1, nt)
    has_mix = s < n_tiles
    has_ffn = s > 0
    ha = ubuf.shape[1] - tile
    hb = zbuf.shape[1] - tile
    off_a = ha - (conv_a_w - 1)

    def mixer_rows(r0, rows):
        p = lambda lo, w: proj_ref[pl.ds(r0, rows), lo:lo + w]
        u = p(0, a_width) * _sigmoid(p(a_width, a_width))
        _slab_store(ubuf, ha + r0, u)
        acc = jnp.broadcast_to(ba_ref[...], (rows, a_width))
        for k in range(conv_a_w):
            acc = acc + _slab_window(ubuf, r0 + off_a + k, rows) * wa_ref[k:k + 1, :]
        ya = _group_a_out(acc, lng_ref[...], lnb_ref[...], beta_a_ref[...])
        mix_ref[pl.ds(r0, rows), 0:a_width] = ya.astype(_BF16)

        o = 2 * a_width
        z = p(o + b_width, b_width) * p(o + 2 * b_width, b_width)
        _slab_store(zbuf, hb + r0, z)
        zb = (_slab_window(zbuf, r0 + hb - 2, rows) * wb_ref[0:1, :]
              + _slab_window(zbuf, r0 + hb - 1, rows) * wb_ref[1:2, :]
              + z * wb_ref[2:3, :])
        yb = p(o, b_width) * zb * beta_b_ref[...]
        mix_ref[pl.ds(r0, rows), a_width:a_width + b_width] = yb.astype(_BF16)

    @pl.when(s == 0)
    def _():
        m = meta_ref.shape[0]
        _slab_store(ubuf, 0, jnp.zeros((ha, a_width), _F32))
        _slab_store(zbuf, 0, jnp.zeros((hb, b_width), _F32))
        h1_ref[0:m, :] = _rmsnorm(meta_ref[...], g1_ref[...]).astype(_BF16)
        proj_ref[0:m, :] = jnp.dot(h1_ref[0:m, :], win_ref[...], preferred_element_type=_F32)
        mixer_rows(0, m)
        xm = meta_ref[...] + jnp.dot(mix_ref[0:m, :], wout_ref[...],
                                     preferred_element_type=_F32)
        hm = _rmsnorm(xm, g2_ref[...]).astype(_BF16)
        for j in range(n_chunks):
            cf0_ref[j] = _ff_up_chunk(hm, wup_ref, j, d_ff)[m - hb:m, :]
        for c in range(ubuf.shape[0]):
            ca0_ref[c] = ubuf[c, m:m + ha, :]
        for c in range(zbuf.shape[0]):
            cb0_ref[c] = zbuf[c, m:m + hb, :]
        h2_ref[...] = jnp.zeros(h2_ref.shape, h2_ref.dtype)
        x1_ref[...] = jnp.zeros(x1_ref.shape, x1_ref.dtype)

    @pl.when(t_mix == 0)
    def _():
        for c in range(ubuf.shape[0]):
            ubuf[c, 0:ha, :] = ca0_ref[c]
        for c in range(zbuf.shape[0]):
            zbuf[c, 0:hb, :] = cb0_ref[c]

    @pl.when(jnp.logical_or(s == 0, t_ffn == 0))
    def _():
        ucar[...] = cf0_ref[...]

    def norm_rows_static(src_ref, g_ref, dst_ref, dtype):
        for r0 in range(0, tile, norm_rows):
            x = src_ref[r0:r0 + norm_rows, :]
            dst_ref[r0:r0 + norm_rows, :] = _rmsnorm(x, g_ref[...]).astype(dtype)

    @pl.when(has_mix)
    def _():
        norm_rows_static(x_ref, g1_ref, h1_ref, _BF16)
        proj_ref[...] = jnp.dot(h1_ref[...], win_ref[...], preferred_element_type=_F32)

    def mixer_chunk(i):
        rows = mix_rows
        mixer_rows(i * rows if isinstance(i, int) else pl.multiple_of(i * rows, rows), rows)

    upbufs = (upbuf0, upbuf1)

    def ffn_up(j, buf):
        _slab_store(buf, 0, ucar[j])
        _slab_store(buf, hb, _ff_up_chunk(h2_ref[...], wup_ref, j, d_ff))
        ucar[j] = _slab_load(buf, tile, hb)

    def ffn_act(j, buf):
        w = _ff_cols(wcf_ref, j, d_ff)
        for r0 in range(0, tile, ffn_rows):
            cv = (_slab_window(buf, r0 + hb - 2, ffn_rows) * w[0:1, :]
                  + _slab_window(buf, r0 + hb - 1, ffn_rows) * w[1:2, :]
                  + _slab_load(buf, r0 + hb, ffn_rows) * w[2:3, :])
            a = _silu(cv[:, 0:FF_CHUNK]) * cv[:, FF_CHUNK:2 * FF_CHUNK]
            act_ref[j, r0:r0 + ffn_rows, :] = a.astype(_BF16)

    def ffn_down(j, n):
        acc = x2_ref[...]
        for i in range(n):
            r0 = (j + i) * FF_CHUNK
            if not isinstance(j, int):
                r0 = pl.multiple_of(r0, FF_CHUNK)
            acc = acc + jnp.dot(act_ref[j + i], wdown_ref[pl.ds(r0, FF_CHUNK), :],
                                preferred_element_type=_F32)
        x2_ref[...] = acc

    n_mix = tile // mix_rows
    n_iter = (n_chunks - 3) // 2
    assert n_chunks >= 3 and mix_first + n_iter * mix_per_pair <= n_mix

    x2_ref[...] = x1_ref[...]
    ffn_up(0, upbuf0)
    ffn_up(1, upbuf1)
    ffn_act(0, upbuf0)
    for i in range(mix_first):
        mixer_chunk(i)

    def ffn_pair(k, c):
        j = 2 * k
        ffn_up(j + 2, upbuf0)
        ffn_act(j + 1, upbuf1)
        ffn_up(j + 3, upbuf1)
        ffn_act(j + 2, upbuf0)
        ffn_down(j, 2)
        for i in range(mix_per_pair):
            mixer_chunk(mix_first + k * mix_per_pair + i)
        return c
    lax.fori_loop(0, n_iter, ffn_pair, 0)
    j_done = 2 * n_iter
    for j in range(j_done + 2, n_chunks):
        ffn_up(j, upbufs[j % 2])
        ffn_act(j - 1, upbufs[(j - 1) % 2])
    ffn_act(n_chunks - 1, upbufs[(n_chunks - 1) % 2])
    ffn_down(j_done, n_chunks - j_done)

    for i in range(mix_first + n_iter * mix_per_pair, n_mix):
        mixer_chunk(i)
    x1_ref[...] = x_ref[...] + jnp.dot(mix_ref[...], wout_ref[...],
                                       preferred_element_type=_F32)
    norm_rows_static(x2_ref, gf_ref, y_ref, _F32)
    norm_rows_static(x1_ref, g2_ref, h2_ref, _BF16)
    _slab_store(ubuf, 0, _slab_load(ubuf, tile, ha))
    _slab_store(zbuf, 0, _slab_load(zbuf, tile, hb))

    @pl.when(jnp.logical_and(has_ffn, t_ffn == nt - 1))
    def _():
        for j in range(n_chunks):
            last = ucar[j, hb - 2:hb, :]
            for half in range(2):
                c0 = half * d_ff + j * FF_CHUNK
                cf_ref[:, c0:c0 + FF_CHUNK] = last[:, half * FF_CHUNK:(half + 1) * FF_CHUNK]

    @pl.when(jnp.logical_and(has_mix, t_mix == nt - 1))
    def _():
        ca_ref[...] = _slab_window(ubuf, off_a, conv_a_w - 1)
        cb_ref[...] = _slab_window(zbuf, hb - 2, 2)


def _prompt_layer(x, meta, wts, *, tile, norm_rows, mix_rows, mix_first, mix_per_pair,
                  ffn_rows):
    nb, length, d = x.shape
    nt = length // tile
    n_tiles = nb * nt
    (g1, win, wa, ba, lng, lnb, wb, beta_a, beta_b, wout, g2, wup, wcf, wdown, gf) = wts
    a_width = wa.shape[1]
    b_width = wb.shape[1]
    conv_a_w = wa.shape[0]
    d_ff = wdown.shape[0]
    n_chunks = d_ff // FF_CHUNK
    ha = -(-(conv_a_w - 1) // SUBLANES) * SUBLANES
    hb = SUBLANES
    assert meta.shape[0] % (2 * SUBLANES) == 0 and hb <= meta.shape[0] <= tile
    kern = functools.partial(_seq_kernel, tile=tile, nt=nt, n_tiles=n_tiles,
                             norm_rows=norm_rows, mix_rows=mix_rows,
                             mix_first=mix_first, mix_per_pair=mix_per_pair, ffn_rows=ffn_rows,
                             a_width=a_width, b_width=b_width, conv_a_w=conv_a_w,
                             d_ff=d_ff)
    mix_tile = lambda s: jnp.minimum(s, n_tiles - 1)
    ffn_tile = lambda s: jnp.maximum(s - 1, 0)
    const_spec = lambda a: pl.BlockSpec(a.shape, lambda s: (0,) * a.ndim,
                                        pipeline_mode=pl.Buffered(1))
    consts = (meta,) + tuple(wts)
    in_specs = [pl.BlockSpec((None, tile, d),
                             lambda s: (mix_tile(s) // nt, mix_tile(s) % nt, 0))]
    in_specs += [const_spec(c) for c in consts]
    state_shapes = ((conv_a_w - 1, a_width), (2, b_width), (2, 2 * d_ff))
    state_tiles = (mix_tile, mix_tile, ffn_tile)
    out_shape = (jax.ShapeDtypeStruct((nb, length, d), _F32),) + tuple(
        jax.ShapeDtypeStruct((nb,) + shp, _F32) for shp in state_shapes)
    out_specs = (
        pl.BlockSpec((None, tile, d), lambda s: (ffn_tile(s) // nt, ffn_tile(s) % nt, 0)),
    ) + tuple(
        pl.BlockSpec((None,) + shp, lambda s, f=f, n=len(shp): (f(s) // nt,) + (0,) * n)
        for shp, f in zip(state_shapes, state_tiles))
    scratch = [
        pltpu.VMEM((a_width // LANES, ha + tile, LANES), _F32),
        pltpu.VMEM((b_width // LANES, hb + tile, LANES), _F32),
        pltpu.VMEM((n_chunks, hb, 2 * FF_CHUNK), _F32),
        pltpu.VMEM((a_width // LANES, ha, LANES), _F32),
        pltpu.VMEM((b_width // LANES, hb, LANES), _F32),
        pltpu.VMEM((n_chunks, hb, 2 * FF_CHUNK), _F32),
        pltpu.VMEM((2 * FF_CHUNK // LANES, hb + tile, LANES), _F32),
        pltpu.VMEM((2 * FF_CHUNK // LANES, hb + tile, LANES), _F32),
        pltpu.VMEM((tile, d), _BF16),
        pltpu.VMEM((tile, d), _BF16),
        pltpu.VMEM((tile, win.shape[1]), _F32),
        pltpu.VMEM((tile, a_width + b_width), _BF16),
        pltpu.VMEM((tile, d), _F32),
        pltpu.VMEM((tile, d), _F32),
        pltpu.VMEM((n_chunks, tile, FF_CHUNK), _BF16),
    ]
    return pl.pallas_call(
        kern,
        grid=(n_tiles + 1,),
        in_specs=in_specs,
        out_specs=out_specs,
        out_shape=out_shape,
        scratch_shapes=scratch,
        compiler_params=pltpu.CompilerParams(
            dimension_semantics=("arbitrary",),
            vmem_limit_bytes=VMEM_LIMIT_BYTES),
        name="prompt_layer",
    )(x, *consts)


def _sample_kernel(xs_ref, sa_ref, sb_ref, sf_ref, g1_ref, win_ref, wa_ref, ba_ref,
                   lng_ref, lnb_ref, wb_ref, beta_a_ref, beta_b_ref, wout_ref, g2_ref,
                   wup_ref, wcf_ref, wdown_ref, gf_ref,
                   ys_ref, na_ref, nb_ref, nf_ref,
                   h_ref, proj_ref, u_ref, z_ref, mix_ref, x1_ref, up_ref, act_ref, hist_ref,
                   pa_ref, *, steps, d, a_width, b_width, conv_a_w, d_ff, n_chunks):
    s = xs_ref.shape[0]
    hist_a = conv_a_w - 1
    rows_of = lambda t: slice(t * s, (t + 1) * s)

    for t in range(steps):
        h_ref[rows_of(t), :] = _rmsnorm(xs_ref[:, t, :], g1_ref[...]).astype(_BF16)
    proj_ref[...] = jnp.dot(h_ref[...], win_ref[...], preferred_element_type=_F32)

    o = 2 * a_width
    for t in range(steps):
        r = rows_of(t)
        u_ref[r, :] = proj_ref[r, 0:a_width] * _sigmoid(proj_ref[r, a_width:o])
        z_ref[r, :] = (proj_ref[r, o + b_width:o + 2 * b_width]
                       * proj_ref[r, o + 2 * b_width:o + 3 * b_width])

    n_slabs = hist_ref.shape[0]
    pad_rows = hist_ref.shape[2] - hist_a
    for c in range(n_slabs):
        hist_ref[c, :, 0:hist_a, :] = sa_ref[:, :, c * LANES:(c + 1) * LANES]
        hist_ref[c, :, hist_a:hist_a + pad_rows, :] = jnp.zeros((s, pad_rows, LANES), _F32)
    for s0 in range(0, s, SUBLANES):
        parts = []
        for c in range(n_slabs):
            acc = jnp.zeros((SUBLANES, steps, LANES), _F32)
            for k in range(hist_a):
                win = hist_ref[c, s0:s0 + SUBLANES, pl.ds(k, steps, stride=1), :]
                acc = acc + win * wa_ref[k:k + 1, c * LANES:(c + 1) * LANES]
            parts.append(acc)
        pa_ref[s0:s0 + SUBLANES, :, :] = jnp.concatenate(parts, axis=2)

    def xe_b(j):
        return sb_ref[:, j, :] if j < 2 else z_ref[rows_of(j - 2), :]

    for t in range(steps):
        r = rows_of(t)
        acc = ba_ref[...] + pa_ref[:, t, :]
        for k in range(hist_a - t, conv_a_w):
            acc = acc + u_ref[rows_of(t + k - hist_a), :] * wa_ref[k:k + 1, :]
        ya = _group_a_out(acc, lng_ref[...], lnb_ref[...], beta_a_ref[...])
        mix_ref[r, 0:a_width] = ya.astype(_BF16)
        zb = (xe_b(t) * wb_ref[0:1, :] + xe_b(t + 1) * wb_ref[1:2, :]
              + xe_b(t + 2) * wb_ref[2:3, :])
        yb = proj_ref[r, o:o + b_width] * zb * beta_b_ref[...]
        mix_ref[r, a_width:a_width + b_width] = yb.astype(_BF16)

    na_ref[:, 0:hist_a - steps, :] = sa_ref[:, steps:hist_a, :]
    for t in range(steps):
        na_ref[:, hist_a - steps + t, :] = u_ref[rows_of(t), :]
    for j in range(2):
        nb_ref[:, j, :] = xe_b(j + steps)

    x1_ref[...] = jnp.dot(mix_ref[...], wout_ref[...], preferred_element_type=_F32)
    for t in range(steps):
        r = rows_of(t)
        x1 = x1_ref[r, :] + xs_ref[:, t, :]
        x1_ref[r, :] = x1
        h_ref[r, :] = _rmsnorm(x1, g2_ref[...]).astype(_BF16)

    for j in range(n_chunks):
        up_ref[...] = _ff_up_chunk(h_ref[...], wup_ref, j, d_ff)
        w = _ff_cols(wcf_ref, j, d_ff)
        lo = j * FF_CHUNK

        def xe_f(i, half):
            if i < 2:
                c0 = half * d_ff + lo
                return sf_ref[:, i, c0:c0 + FF_CHUNK]
            return up_ref[rows_of(i - 2), half * FF_CHUNK:(half + 1) * FF_CHUNK]

        for t in range(steps):
            gv = []
            for half in range(2):
                wh = w[:, half * FF_CHUNK:(half + 1) * FF_CHUNK]
                gv.append(xe_f(t, half) * wh[0:1, :] + xe_f(t + 1, half) * wh[1:2, :]
                          + xe_f(t + 2, half) * wh[2:3, :])
            act_ref[j, rows_of(t), :] = (_silu(gv[0]) * gv[1]).astype(_BF16)
        for i in range(2):
            for half in range(2):
                c0 = half * d_ff + lo
                nf_ref[:, i, c0:c0 + FF_CHUNK] = xe_f(i + steps, half)

    acc = _ff_down_chain(x1_ref[...], act_ref, wdown_ref, n_chunks)
    for t in range(steps):
        ys_ref[:, t, :] = _rmsnorm(acc[rows_of(t), :], gf_ref[...])


def _sample_layer(xs, sa, sb, sf, wts, *, steps, seq_block):
    n = xs.shape[0]
    (g1, win, wa, ba, lng, lnb, wb, beta_a, beta_b, wout, g2, wup, wcf, wdown, gf) = wts
    d = g1.shape[1]
    a_width = wa.shape[1]
    b_width = wb.shape[1]
    conv_a_w = wa.shape[0]
    d_ff = wdown.shape[0]
    n_chunks = d_ff // FF_CHUNK
    m = steps * seq_block
    kern = functools.partial(_sample_kernel, steps=steps, d=d, a_width=a_width,
                             b_width=b_width, conv_a_w=conv_a_w, d_ff=d_ff,
                             n_chunks=n_chunks)
    row_spec = lambda a: pl.BlockSpec((seq_block,) + a.shape[1:], lambda i: (i, 0, 0))
    const_spec = lambda a: pl.BlockSpec(a.shape, lambda i: (0,) * a.ndim,
                                        pipeline_mode=pl.Buffered(1))
    data = (xs, sa, sb, sf)
    in_specs = [row_spec(a) for a in data] + [const_spec(a) for a in wts]
    out_shape = tuple(jax.ShapeDtypeStruct(a.shape, _F32) for a in data)
    out_specs = tuple(row_spec(a) for a in data)
    scratch = [
        pltpu.VMEM((m, d), _BF16),
        pltpu.VMEM((m, win.shape[1]), _F32),
        pltpu.VMEM((m, a_width), _F32),
        pltpu.VMEM((m, b_width), _F32),
        pltpu.VMEM((m, a_width + b_width), _BF16),
        pltpu.VMEM((m, d), _F32),
        pltpu.VMEM((m, 2 * FF_CHUNK), _F32),
        pltpu.VMEM((n_chunks, m, FF_CHUNK), _BF16),
        pltpu.VMEM((a_width // LANES, seq_block, conv_a_w - 1 + steps + 2, LANES), _F32),
        pltpu.VMEM((seq_block, steps, a_width), _F32),
    ]
    return pl.pallas_call(
        kern,
        grid=(n // seq_block,),
        in_specs=in_specs,
        out_specs=out_specs,
        out_shape=out_shape,
        scratch_shapes=scratch,
        compiler_params=pltpu.CompilerParams(
            dimension_semantics=("arbitrary",),
            vmem_limit_bytes=VMEM_LIMIT_BYTES),
        name="sample_layer",
    )(*data, *wts)


def kernel(x_prompt, x_sample, state_conv_a, state_conv_b, state_conv_ffn, meta_tokens,
           norm_mix_g, w_in, w_conv_a, b_conv_a, gn_a_g, gn_a_b, w_conv_b, beta_a, beta_b,
           w_out, norm_ffn_g, w_up, w_conv_f, w_down, norm_final_g):
    depth = w_in.shape[0]
    assert depth == 1, "single-layer step only"
    batch, seq, d = x_prompt.shape
    n_dec, dec_seq, _ = x_sample.shape
    conv_a_w, a_width = w_conv_a.shape[1:]
    b_width = w_conv_b.shape[2]
    d_ff = w_down.shape[1]
    assert a_width % LANES == 0 and LANES == 2 * HEAD_DIM
    assert d_ff % FF_CHUNK == 0 and w_conv_b.shape[1] == 3 and w_conv_f.shape[1] == 3

    row = lambda v: v.reshape(1, -1)
    wts = (
        row(norm_mix_g[0]),
        w_in[0].astype(_BF16),
        w_conv_a[0], row(b_conv_a[0]), row(gn_a_g[0]), row(gn_a_b[0]),
        w_conv_b[0], row(beta_a[0]), row(beta_b[0]),
        w_out[0].astype(_BF16),
        row(norm_ffn_g[0]),
        w_up[0].astype(_BF16),
        w_conv_f[0],
        w_down[0].astype(_BF16),
        row(norm_final_g),
    )

    y_prompt, na_p, nb_p, nf_p = _prompt_layer(
        x_prompt, meta_tokens, wts, tile=512, norm_rows=64, mix_rows=32,
        mix_first=3, mix_per_pair=2, ffn_rows=32)

    y_s, na_s, nb_s, nf_s = _sample_layer(
        x_sample, state_conv_a[0], state_conv_b[0], state_conv_ffn[0],
        wts, steps=dec_seq, seq_block=32)

    return (y_prompt, y_s, na_p[None], nb_p[None], nf_p[None], na_s[None], nb_s[None],
            nf_s[None])
```

```python
import functools

import jax
import jax.numpy as jnp
from jax import lax
from jax.experimental import pallas as pl
from jax.experimental.pallas import tpu as pltpu

EPS = 1e-6
HEAD_DIM = 64
LANES = 128
SUBLANES = 8
FF_CHUNK = 256
VMEM_LIMIT_BYTES = 56 * 1024 * 1024

_F32 = jnp.float32
_BF16 = jnp.bfloat16


def _rmsnorm(x, g):
    ms = jnp.mean(x * x, axis=-1, keepdims=True)
    return x * lax.rsqrt(ms + EPS) * g


def _sigmoid(x):
    return 1.0 / (1.0 + jnp.exp(-x))


def _silu(x):
    return x * _sigmoid(x)


def _group_mean(v, lane_lo):
    sa = jnp.sum(jnp.where(lane_lo, v, 0.0), axis=1, keepdims=True)
    sb = jnp.sum(jnp.where(lane_lo, 0.0, v), axis=1, keepdims=True)
    return jnp.where(lane_lo, sa, sb) * (1.0 / HEAD_DIM)


def _head_layernorm(ua):
    rows, width = ua.shape
    lane_lo = lax.broadcasted_iota(jnp.int32, (rows, LANES), 1) < HEAD_DIM
    outs = []
    for j in range(width // LANES):
        xb = ua[:, j * LANES:(j + 1) * LANES]
        d = xb - _group_mean(xb, lane_lo)
        var = _group_mean(d * d, lane_lo)
        outs.append(d * lax.rsqrt(var + EPS))
    return jnp.concatenate(outs, axis=1)


def _group_a_out(ua, lng, lnb, beta_a):
    return _silu(_head_layernorm(ua) * lng + lnb) * beta_a


def _rows_times(x, w8):
    r, c = x.shape
    return (x.reshape(r // SUBLANES, SUBLANES, c) * w8).reshape(r, c)


def _ff_cols(ref, j, d_ff):
    off = j * FF_CHUNK
    if not isinstance(j, int):
        off = pl.multiple_of(off, FF_CHUNK)
    return jnp.concatenate([ref[:, :, pl.ds(off, FF_CHUNK)],
                            ref[:, :, pl.ds(d_ff + off, FF_CHUNK)]], axis=2)


def _ff_up_chunk(h, wup_ref, j, d_ff):
    off = j * FF_CHUNK
    if not isinstance(j, int):
        off = pl.multiple_of(off, FF_CHUNK)
    g = jnp.dot(h, wup_ref[:, pl.ds(off, FF_CHUNK)], preferred_element_type=_F32)
    v = jnp.dot(h, wup_ref[:, pl.ds(d_ff + off, FF_CHUNK)], preferred_element_type=_F32)
    return jnp.concatenate([g, v], axis=1)


def _ff_down_chain(acc, act_ref, wdown_ref, n_chunks):
    for j in range(n_chunks):
        acc = acc + jnp.dot(act_ref[j], wdown_ref[j * FF_CHUNK:(j + 1) * FF_CHUNK, :],
                            preferred_element_type=_F32)
    return acc


def _slab_store(buf, row0, val):
    for c in range(buf.shape[0]):
        buf[c, pl.ds(row0, val.shape[0]), :] = val[:, c * LANES:(c + 1) * LANES]


def _slab_load(buf, row0, nrows):
    return jnp.concatenate(
        [buf[c, pl.ds(row0, nrows), :] for c in range(buf.shape[0])], axis=1)


def _slab_window(buf, row0, nrows):
    return jnp.concatenate(
        [buf[c, pl.ds(row0, nrows, stride=1), :] for c in range(buf.shape[0])], axis=1)


def _seq_kernel(x_ref, meta_ref, g1_ref, win_ref, wa_ref, ba_ref,
                lng_ref, lnb_ref, wb_ref, beta_a_ref, beta_b_ref, wout_ref, g2_ref,
                wup_ref, wcf_ref, wdown_ref, gf_ref,
                y_ref, ca_ref, cb_ref, cf_ref,
                ubuf, zbuf, ucar, ca0_ref, cb0_ref, cf0_ref, upbuf0, upbuf1, h1_ref, h2_ref,
                proj_ref, mix_ref, x1_ref, x2_ref, act_ref,
                *, tile, nt, n_tiles, norm_rows, mix_rows, mix_first, mix_per_pair, ffn_rows,
                a_width, b_width, conv_a_w, d_ff):
    n_chunks = d_ff // FF_CHUNK
    s = pl.program_id(0)
    t_mix = lax.rem(s, nt)
    t_ffn = lax.rem(s + nt - 1, nt)
    has_mix = s < n_tiles
    has_ffn = s > 0
    ha = ubuf.shape[1] - tile
    hb = zbuf.shape[1] - tile
    off_a = ha - (conv_a_w - 1)

    def mixer_rows(r0, rows):
        p = lambda lo, w: proj_ref[pl.ds(r0, rows), lo:lo + w]
        u = p(0, a_width) * _sigmoid(p(a_width, a_width))
        _slab_store(ubuf, ha + r0, u)
        acc = jnp.broadcast_to(ba_ref[...], (rows, a_width))
        for k in range(conv_a_w):
            acc = acc + _rows_times(_slab_window(ubuf, r0 + off_a + k, rows), wa_ref[k])
        ya = _group_a_out(acc, lng_ref[...], lnb_ref[...], beta_a_ref[...])
        mix_ref[pl.ds(r0, rows), 0:a_width] = ya.astype(_BF16)

        o = 2 * a_width
        z = p(o + b_width, b_width) * p(o + 2 * b_width, b_width)
        _slab_store(zbuf, hb + r0, z)
        zb = (_rows_times(_slab_window(zbuf, r0 + hb - 2, rows), wb_ref[0])
              + _rows_times(_slab_window(zbuf, r0 + hb - 1, rows), wb_ref[1])
              + _rows_times(z, wb_ref[2]))
        yb = p(o, b_width) * zb * beta_b_ref[...]
        mix_ref[pl.ds(r0, rows), a_width:a_width + b_width] = yb.astype(_BF16)

    @pl.when(s == 0)
    def _():
        m = meta_ref.shape[0]
        _slab_store(ubuf, 0, jnp.zeros((ha, a_width), _F32))
        _slab_store(zbuf, 0, jnp.zeros((hb, b_width), _F32))
        h1_ref[0:m, :] = _rmsnorm(meta_ref[...], g1_ref[...]).astype(_BF16)
        proj_ref[0:m, :] = jnp.dot(h1_ref[0:m, :], win_ref[...], preferred_element_type=_F32)
        mixer_rows(0, m)
        xm = meta_ref[...] + jnp.dot(mix_ref[0:m, :], wout_ref[...],
                                     preferred_element_type=_F32)
        hm = _rmsnorm(xm, g2_ref[...]).astype(_BF16)
        for j in range(n_chunks):
            cf0_ref[j] = _ff_up_chunk(hm, wup_ref, j, d_ff)[m - hb:m, :]
        for c in range(ubuf.shape[0]):
            ca0_ref[c] = ubuf[c, m:m + ha, :]
        for c in range(zbuf.shape[0]):
            cb0_ref[c] = zbuf[c, m:m + hb, :]
        h2_ref[...] = jnp.zeros(h2_ref.shape, h2_ref.dtype)
        x1_ref[...] = jnp.zeros(x1_ref.shape, x1_ref.dtype)

    @pl.when(t_mix == 0)
    def _():
        for c in range(ubuf.shape[0]):
            ubuf[c, 0:ha, :] = ca0_ref[c]
        for c in range(zbuf.shape[0]):
            zbuf[c, 0:hb, :] = cb0_ref[c]

    @pl.when(jnp.logical_or(s == 0, t_ffn == 0))
    def _():
        ucar[...] = cf0_ref[...]

    def norm_rows_static(src_ref, g_ref, dst_ref, dtype):
        for r0 in range(0, tile, norm_rows):
            x = src_ref[r0:r0 + norm_rows, :]
            dst_ref[r0:r0 + norm_rows, :] = _rmsnorm(x, g_ref[...]).astype(dtype)

    @pl.when(has_mix)
    def _():
        norm_rows_static(x_ref, g1_ref, h1_ref, _BF16)
        proj_ref[...] = jnp.dot(h1_ref[...], win_ref[...], preferred_element_type=_F32)

    def mixer_chunk(i):
        rows = mix_rows
        mixer_rows(i * rows if isinstance(i, int) else pl.multiple_of(i * rows, rows), rows)

    upbufs = (upbuf0, upbuf1)

    def ffn_up(j, buf):
        _slab_store(buf, 0, ucar[j])
        _slab_store(buf, hb, _ff_up_chunk(h2_ref[...], wup_ref, j, d_ff))
        ucar[j] = _slab_load(buf, tile, hb)

    def ffn_act(j, buf):
        w = _ff_cols(wcf_ref, j, d_ff)
        for r0 in range(0, tile, ffn_rows):
            cv = (_rows_times(_slab_window(buf, r0 + hb - 2, ffn_rows), w[0])
                  + _rows_times(_slab_window(buf, r0 + hb - 1, ffn_rows), w[1])
                  + _rows_times(_slab_load(buf, r0 + hb, ffn_rows), w[2]))
            a = _silu(cv[:, 0:FF_CHUNK]) * cv[:, FF_CHUNK:2 * FF_CHUNK]
            act_ref[j, r0:r0 + ffn_rows, :] = a.astype(_BF16)

    def ffn_down(j, n):
        acc = x2_ref[...]
        for i in range(n):
            r0 = (j + i) * FF_CHUNK
            if not isinstance(j, int):
                r0 = pl.multiple_of(r0, FF_CHUNK)
            acc = acc + jnp.dot(act_ref[j + i], wdown_ref[pl.ds(r0, FF_CHUNK), :],
                                preferred_element_type=_F32)
        x2_ref[...] = acc

    n_mix = tile // mix_rows
    n_iter = (n_chunks - 3) // 2
    assert n_chunks >= 3 and mix_first + n_iter * mix_per_pair <= n_mix

    x2_ref[...] = x1_ref[...]
    ffn_up(0, upbuf0)
    ffn_up(1, upbuf1)
    ffn_act(0, upbuf0)
    for i in range(mix_first):
        mixer_chunk(i)

    def ffn_pair(k, c):
        j = 2 * k
        ffn_up(j + 2, upbuf0)
        ffn_act(j + 1, upbuf1)
        ffn_up(j + 3, upbuf1)
        ffn_act(j + 2, upbuf0)
        ffn_down(j, 2)
        for i in range(mix_per_pair):
            mixer_chunk(mix_first + k * mix_per_pair + i)
        return c
    lax.fori_loop(0, n_iter, ffn_pair, 0)
    j_done = 2 * n_iter
    for j in range(j_done + 2, n_chunks):
        ffn_up(j, upbufs[j % 2])
        ffn_act(j - 1, upbufs[(j - 1) % 2])
    ffn_act(n_chunks - 1, upbufs[(n_chunks - 1) % 2])
    ffn_down(j_done, n_chunks - j_done)

    for i in range(mix_first + n_iter * mix_per_pair, n_mix):
        mixer_chunk(i)
    x1_ref[...] = x_ref[...] + jnp.dot(mix_ref[...], wout_ref[...],
                                       preferred_element_type=_F32)
    norm_rows_static(x2_ref, gf_ref, y_ref, _F32)
    norm_rows_static(x1_ref, g2_ref, h2_ref, _BF16)
    _slab_store(ubuf, 0, _slab_load(ubuf, tile, ha))
    _slab_store(zbuf, 0, _slab_load(zbuf, tile, hb))

    @pl.when(jnp.logical_and(has_ffn, t_ffn == nt - 1))
    def _():
        for j in range(n_chunks):
            last = ucar[j, hb - 2:hb, :]
            for half in range(2):
                c0 = half * d_ff + j * FF_CHUNK
                cf_ref[:, c0:c0 + FF_CHUNK] = last[:, half * FF_CHUNK:(half + 1) * FF_CHUNK]

    @pl.when(jnp.logical_and(has_mix, t_mix == nt - 1))
    def _():
        ca_ref[...] = _slab_window(ubuf, off_a, conv_a_w - 1)
        cb_ref[...] = _slab_window(zbuf, hb - 2, 2)


def _prompt_layer(x, meta, wts, *, tile, norm_rows, mix_rows, mix_first, mix_per_pair,
                  ffn_rows):
    nb, length, d = x.shape
    nt = length // tile
    n_tiles = nb * nt
    (g1, win, wa, ba, lng, lnb, wb, beta_a, beta_b, wout, g2, wup, wcf, wdown, gf) = wts
    a_width = wa.shape[-1]
    b_width = wb.shape[-1]
    conv_a_w = wa.shape[0]
    d_ff = wdown.shape[0]
    n_chunks = d_ff // FF_CHUNK
    ha = -(-(conv_a_w - 1) // SUBLANES) * SUBLANES
    hb = SUBLANES
    assert meta.shape[0] % (2 * SUBLANES) == 0 and hb <= meta.shape[0] <= tile
    kern = functools.partial(_seq_kernel, tile=tile, nt=nt, n_tiles=n_tiles,
                             norm_rows=norm_rows, mix_rows=mix_rows,
                             mix_first=mix_first, mix_per_pair=mix_per_pair, ffn_rows=ffn_rows,
                             a_width=a_width, b_width=b_width, conv_a_w=conv_a_w,
                             d_ff=d_ff)
    mix_tile = lambda s: jnp.minimum(s, n_tiles - 1)
    ffn_tile = lambda s: jnp.maximum(s - 1, 0)
    const_spec = lambda a: pl.BlockSpec(a.shape, lambda s: (0,) * a.ndim,
                                        pipeline_mode=pl.Buffered(1))
    consts = (meta,) + tuple(wts)
    in_specs = [pl.BlockSpec((None, tile, d),
                             lambda s: (mix_tile(s) // nt, mix_tile(s) % nt, 0))]
    in_specs += [const_spec(c) for c in consts]
    state_shapes = ((conv_a_w - 1, a_width), (2, b_width), (2, 2 * d_ff))
    state_tiles = (mix_tile, mix_tile, ffn_tile)
    out_shape = (jax.ShapeDtypeStruct((nb, length, d), _F32),) + tuple(
        jax.ShapeDtypeStruct((nb,) + shp, _F32) for shp in state_shapes)
    out_specs = (
        pl.BlockSpec((None, tile, d), lambda s: (ffn_tile(s) // nt, ffn_tile(s) % nt, 0)),
    ) + tuple(
        pl.BlockSpec((None,) + shp, lambda s, f=f, n=len(shp): (f(s) // nt,) + (0,) * n)
        for shp, f in zip(state_shapes, state_tiles))
    scratch = [
        pltpu.VMEM((a_width // LANES, ha + tile, LANES), _F32),
        pltpu.VMEM((b_width // LANES, hb + tile, LANES), _F32),
        pltpu.VMEM((n_chunks, hb, 2 * FF_CHUNK), _F32),
        pltpu.VMEM((a_width // LANES, ha, LANES), _F32),
        pltpu.VMEM((b_width // LANES, hb, LANES), _F32),
        pltpu.VMEM((n_chunks, hb, 2 * FF_CHUNK), _F32),
        pltpu.VMEM((2 * FF_CHUNK // LANES, hb + tile, LANES), _F32),
        pltpu.VMEM((2 * FF_CHUNK // LANES, hb + tile, LANES), _F32),
        pltpu.VMEM((tile, d), _BF16),
        pltpu.VMEM((tile, d), _BF16),
        pltpu.VMEM((tile, win.shape[1]), _F32),
        pltpu.VMEM((tile, a_width + b_width), _BF16),
        pltpu.VMEM((tile, d), _F32),
        pltpu.VMEM((tile, d), _F32),
        pltpu.VMEM((n_chunks, tile, FF_CHUNK), _BF16),
    ]
    return pl.pallas_call(
        kern,
        grid=(n_tiles + 1,),
        in_specs=in_specs,
        out_specs=out_specs,
        out_shape=out_shape,
        scratch_shapes=scratch,
        compiler_params=pltpu.CompilerParams(
            dimension_semantics=("arbitrary",),
            vmem_limit_bytes=VMEM_LIMIT_BYTES),
        name="prompt_layer",
    )(x, *consts)


def _sample_kernel(xs_ref, sa_ref, sb_ref, sf_ref, g1_ref, win_ref, wa_ref, ba_ref,
                   lng_ref, lnb_ref, wb_ref, beta_a_ref, beta_b_ref, wout_ref, g2_ref,
                   wup_ref, wcf_ref, wdown_ref, gf_ref,
                   ys_ref, na_ref, nb_ref, nf_ref,
                   h_ref, proj_ref, u_ref, z_ref, mix_ref, x1_ref, up_ref, act_ref,
                   *, steps, d, a_width, b_width, conv_a_w, d_ff, n_chunks):
    s = xs_ref.shape[0]
    hist_a = conv_a_w - 1
    rows_of = lambda t: slice(t * s, (t + 1) * s)

    for t in range(steps):
        h_ref[rows_of(t), :] = _rmsnorm(xs_ref[:, t, :], g1_ref[...]).astype(_BF16)
    proj_ref[...] = jnp.dot(h_ref[...], win_ref[...], preferred_element_type=_F32)

    o = 2 * a_width
    for t in range(steps):
        r = rows_of(t)
        u_ref[r, :] = proj_ref[r, 0:a_width] * _sigmoid(proj_ref[r, a_width:o])
        z_ref[r, :] = (proj_ref[r, o + b_width:o + 2 * b_width]
                       * proj_ref[r, o + 2 * b_width:o + 3 * b_width])

    def xe_a(j):
        return sa_ref[j] if j < hist_a else u_ref[rows_of(j - hist_a), :]

    def xe_b(j):
        return sb_ref[:, j, :] if j < 2 else z_ref[rows_of(j - 2), :]

    for t in range(steps):
        r = rows_of(t)
        acc = jnp.broadcast_to(ba_ref[...], (s, a_width))
        for k in range(conv_a_w):
            acc = acc + _rows_times(xe_a(t + k), wa_ref[k])
        ya = _group_a_out(acc, lng_ref[...], lnb_ref[...], beta_a_ref[...])
        mix_ref[r, 0:a_width] = ya.astype(_BF16)
        zb = (_rows_times(xe_b(t), wb_ref[0]) + _rows_times(xe_b(t + 1), wb_ref[1])
              + _rows_times(xe_b(t + 2), wb_ref[2]))
        yb = proj_ref[r, o:o + b_width] * zb * beta_b_ref[...]
        mix_ref[r, a_width:a_width + b_width] = yb.astype(_BF16)

    for j in range(hist_a):
        na_ref[j] = xe_a(j + steps)
    for j in range(2):
        nb_ref[:, j, :] = xe_b(j + steps)

    x1_ref[...] = jnp.dot(mix_ref[...], wout_ref[...], preferred_element_type=_F32)
    for t in range(steps):
        r = rows_of(t)
        x1 = x1_ref[r, :] + xs_ref[:, t, :]
        x1_ref[r, :] = x1
        h_ref[r, :] = _rmsnorm(x1, g2_ref[...]).astype(_BF16)

    for j in range(n_chunks):
        up_ref[...] = _ff_up_chunk(h_ref[...], wup_ref, j, d_ff)
        w = _ff_cols(wcf_ref, j, d_ff)
        lo = j * FF_CHUNK

        def xe_f(i, half):
            if i < 2:
                c0 = half * d_ff + lo
                return sf_ref[:, i, c0:c0 + FF_CHUNK]
            return up_ref[rows_of(i - 2), half * FF_CHUNK:(half + 1) * FF_CHUNK]

        for t in range(steps):
            gv = []
            for half in range(2):
                wh = w[:, :, half * FF_CHUNK:(half + 1) * FF_CHUNK]
                gv.append(_rows_times(xe_f(t, half), wh[0]) + _rows_times(xe_f(t + 1, half), wh[1])
                          + _rows_times(xe_f(t + 2, half), wh[2]))
            act_ref[j, rows_of(t), :] = (_silu(gv[0]) * gv[1]).astype(_BF16)
        for i in range(2):
            for half in range(2):
                c0 = half * d_ff + lo
                nf_ref[:, i, c0:c0 + FF_CHUNK] = xe_f(i + steps, half)

    acc = _ff_down_chain(x1_ref[...], act_ref, wdown_ref, n_chunks)
    for t in range(steps):
        ys_ref[:, t, :] = _rmsnorm(acc[rows_of(t), :], gf_ref[...])


def _sample_layer(xs, sa, sb, sf, wts, *, steps, seq_block):
    n = xs.shape[0]
    (g1, win, wa, ba, lng, lnb, wb, beta_a, beta_b, wout, g2, wup, wcf, wdown, gf) = wts
    d = g1.shape[1]
    a_width = wa.shape[-1]
    b_width = wb.shape[-1]
    conv_a_w = wa.shape[0]
    d_ff = wdown.shape[0]
    n_chunks = d_ff // FF_CHUNK
    m = steps * seq_block
    kern = functools.partial(_sample_kernel, steps=steps, d=d, a_width=a_width,
                             b_width=b_width, conv_a_w=conv_a_w, d_ff=d_ff,
                             n_chunks=n_chunks)
    seq_spec = lambda a: pl.BlockSpec((seq_block,) + a.shape[1:], lambda i: (i, 0, 0))
    row_spec = lambda a: (pl.BlockSpec((a.shape[0], seq_block, a.shape[2]), lambda i: (0, i, 0))
                          if a is sa else seq_spec(a))
    const_spec = lambda a: pl.BlockSpec(a.shape, lambda i: (0,) * a.ndim,
                                        pipeline_mode=pl.Buffered(1))
    data = (xs, sa, sb, sf)
    in_specs = [row_spec(a) for a in data] + [const_spec(a) for a in wts]
    out_shape = tuple(jax.ShapeDtypeStruct(a.shape, _F32) for a in data)
    out_specs = tuple(row_spec(a) for a in data)
    scratch = [
        pltpu.VMEM((m, d), _BF16),
        pltpu.VMEM((m, win.shape[1]), _F32),
        pltpu.VMEM((m, a_width), _F32),
        pltpu.VMEM((m, b_width), _F32),
        pltpu.VMEM((m, a_width + b_width), _BF16),
        pltpu.VMEM((m, d), _F32),
        pltpu.VMEM((m, 2 * FF_CHUNK), _F32),
        pltpu.VMEM((n_chunks, m, FF_CHUNK), _BF16),
    ]
    return pl.pallas_call(
        kern,
        grid=(n // seq_block,),
        in_specs=in_specs,
        out_specs=out_specs,
        out_shape=out_shape,
        scratch_shapes=scratch,
        compiler_params=pltpu.CompilerParams(
            dimension_semantics=("arbitrary",),
            vmem_limit_bytes=VMEM_LIMIT_BYTES),
        name="sample_layer",
    )(*data, *wts)


def kernel(x_prompt, x_sample, state_conv_a, state_conv_b, state_conv_ffn, meta_tokens,
           norm_mix_g, w_in, w_conv_a, b_conv_a, gn_a_g, gn_a_b, w_conv_b, beta_a, beta_b,
           w_out, norm_ffn_g, w_up, w_conv_f, w_down, norm_final_g):
    depth = w_in.shape[0]
    assert depth == 1, "single-layer step only"
    batch, seq, d = x_prompt.shape
    n_dec, dec_seq, _ = x_sample.shape
    conv_a_w, a_width = w_conv_a.shape[1:]
    b_width = w_conv_b.shape[2]
    d_ff = w_down.shape[1]
    assert a_width % LANES == 0 and LANES == 2 * HEAD_DIM
    assert d_ff % FF_CHUNK == 0 and w_conv_b.shape[1] == 3 and w_conv_f.shape[1] == 3

    row = lambda v: v.reshape(1, -1)
    tile8 = lambda w: jnp.broadcast_to(w[:, None, :], (w.shape[0], SUBLANES, w.shape[1]))
    wts = (
        row(norm_mix_g[0]),
        w_in[0].astype(_BF16),
        tile8(w_conv_a[0]), row(b_conv_a[0]), row(gn_a_g[0]), row(gn_a_b[0]),
        tile8(w_conv_b[0]), row(beta_a[0]), row(beta_b[0]),
        w_out[0].astype(_BF16),
        row(norm_ffn_g[0]),
        w_up[0].astype(_BF16),
        tile8(w_conv_f[0]),
        w_down[0].astype(_BF16),
        row(norm_final_g),
    )

    y_prompt, na_p, nb_p, nf_p = _prompt_layer(
        x_prompt, meta_tokens, wts, tile=512, norm_rows=64, mix_rows=32,
        mix_first=3, mix_per_pair=2, ffn_rows=32)

    y_s, na_s, nb_s, nf_s = _sample_layer(
        x_sample, jnp.transpose(state_conv_a[0], (1, 0, 2)), state_conv_b[0],
        state_conv_ffn[0], wts, steps=dec_seq, seq_block=32)
    na_s = jnp.transpose(na_s, (1, 0, 2))

    return (y_prompt, y_s, na_p[None], nb_p[None], nf_p[None], na_s[None], nb_s[None],
            nf_s[None])
```

```python
import functools

import jax
import jax.numpy as jnp
from jax import lax
from jax.experimental import pallas as pl
from jax.experimental.pallas import tpu as pltpu

EPS = 1e-6
HEAD_DIM = 64
LANES = 128
SUBLANES = 8
FF_CHUNK = 256
VMEM_LIMIT_BYTES = 56 * 1024 * 1024

PROMPT_TILING = dict(tile=512, norm_rows=64, mix_rows=32, mix_first=5, mix_per_pair=2,
                     ffn_rows=32)
SAMPLE_SEQ_BLOCK = 32

_F32 = jnp.float32
_BF16 = jnp.bfloat16


def _rmsnorm(x, g):
    ms = jnp.mean(x * x, axis=-1, keepdims=True)
    return x * lax.rsqrt(ms + EPS) * g


def _sigmoid(x):
    return 1.0 / (1.0 + jnp.exp(-x))


def _silu(x):
    return x * _sigmoid(x)


def _group_mean(v, lane_lo):
    sa = jnp.sum(jnp.where(lane_lo, v, 0.0), axis=1, keepdims=True)
    sb = jnp.sum(jnp.where(lane_lo, 0.0, v), axis=1, keepdims=True)
    return jnp.where(lane_lo, sa, sb) * (1.0 / HEAD_DIM)


def _head_layernorm(ua):
    rows, width = ua.shape
    lane_lo = lax.broadcasted_iota(jnp.int32, (rows, LANES), 1) < HEAD_DIM
    outs = []
    for j in range(width // LANES):
        xb = ua[:, j * LANES:(j + 1) * LANES]
        d = xb - _group_mean(xb, lane_lo)
        var = _group_mean(d * d, lane_lo)
        outs.append(d * lax.rsqrt(var + EPS))
    return jnp.concatenate(outs, axis=1)


def _group_a_out(ua, lng, lnb, beta_a):
    return _silu(_head_layernorm(ua) * lng + lnb) * beta_a


def _rows_times(x, w8):
    r, c = x.shape
    return (x.reshape(r // SUBLANES, SUBLANES, c) * w8).reshape(r, c)


def _ff_cols(ref, j, d_ff):
    off = j * FF_CHUNK
    if not isinstance(j, int):
        off = pl.multiple_of(off, FF_CHUNK)
    return jnp.concatenate([ref[:, :, pl.ds(off, FF_CHUNK)],
                            ref[:, :, pl.ds(d_ff + off, FF_CHUNK)]], axis=2)


def _ff_up_chunk(h, wup_ref, j, d_ff):
    off = j * FF_CHUNK
    if not isinstance(j, int):
        off = pl.multiple_of(off, FF_CHUNK)
    g = jnp.dot(h, wup_ref[:, pl.ds(off, FF_CHUNK)], preferred_element_type=_F32)
    v = jnp.dot(h, wup_ref[:, pl.ds(d_ff + off, FF_CHUNK)], preferred_element_type=_F32)
    return jnp.concatenate([g, v], axis=1)


def _ff_down_chain(acc, act_ref, wdown_ref, n_chunks):
    for j in range(n_chunks):
        acc = acc + jnp.dot(act_ref[j], wdown_ref[j * FF_CHUNK:(j + 1) * FF_CHUNK, :],
                            preferred_element_type=_F32)
    return acc


def _slab_store(buf, row0, val):
    for c in range(buf.shape[0]):
        buf[c, pl.ds(row0, val.shape[0]), :] = val[:, c * LANES:(c + 1) * LANES]


def _slab_load(buf, row0, nrows):
    return jnp.concatenate(
        [buf[c, pl.ds(row0, nrows), :] for c in range(buf.shape[0])], axis=1)


def _slab_window(buf, row0, nrows):
    return jnp.concatenate(
        [buf[c, pl.ds(row0, nrows, stride=1), :] for c in range(buf.shape[0])], axis=1)


def _seq_kernel(x_ref, meta_ref, g1_ref, win_ref, wa_ref, ba_ref,
                lng_ref, lnb_ref, wb_ref, beta_a_ref, beta_b_ref, wout_ref, g2_ref,
                wup_ref, wcf_ref, wdown_ref, gf_ref,
                y_ref, ca_ref, cb_ref, cf_ref,
                ubuf, zbuf, ucar, ca0_ref, cb0_ref, cf0_ref, upbuf0, upbuf1, h1_ref, h2_ref,
                proj_ref, mix_ref, x1_ref, x2_ref, act_ref,
                *, tile, nt, n_tiles, norm_rows, mix_rows, mix_first, mix_per_pair, ffn_rows,
                a_width, b_width, conv_a_w, d_ff):
    n_chunks = d_ff // FF_CHUNK
    s = pl.program_id(0)
    t_mix = lax.rem(s, nt)
    t_ffn = lax.rem(s + nt - 1, nt)
    has_mix = s < n_tiles
    has_ffn = s > 0
    ha = ubuf.shape[1] - tile
    hb = zbuf.shape[1] - tile
    off_a = ha - (conv_a_w - 1)

    def mixer_rows(r0, rows):
        p = lambda lo, w: proj_ref[pl.ds(r0, rows), lo:lo + w]
        u = p(0, a_width) * _sigmoid(p(a_width, a_width))
        _slab_store(ubuf, ha + r0, u)
        acc = jnp.broadcast_to(ba_ref[...], (rows, a_width))
        for k in range(conv_a_w):
            acc = acc + _rows_times(_slab_window(ubuf, r0 + off_a + k, rows), wa_ref[k])
        ya = _group_a_out(acc, lng_ref[...], lnb_ref[...], beta_a_ref[...])
        mix_ref[pl.ds(r0, rows), 0:a_width] = ya.astype(_BF16)

        o = 2 * a_width
        z = p(o + b_width, b_width) * p(o + 2 * b_width, b_width)
        _slab_store(zbuf, hb + r0, z)
        zb = (_rows_times(_slab_window(zbuf, r0 + hb - 2, rows), wb_ref[0])
              + _rows_times(_slab_window(zbuf, r0 + hb - 1, rows), wb_ref[1])
              + _rows_times(z, wb_ref[2]))
        yb = p(o, b_width) * zb * beta_b_ref[...]
        mix_ref[pl.ds(r0, rows), a_width:a_width + b_width] = yb.astype(_BF16)

    @pl.when(s == 0)
    def _():
        m = meta_ref.shape[0]
        _slab_store(ubuf, 0, jnp.zeros((ha, a_width), _F32))
        _slab_store(zbuf, 0, jnp.zeros((hb, b_width), _F32))
        h1_ref[0:m, :] = _rmsnorm(meta_ref[...], g1_ref[...]).astype(_BF16)
        proj_ref[0:m, :] = jnp.dot(h1_ref[0:m, :], win_ref[...], preferred_element_type=_F32)
        mixer_rows(0, m)
        xm = meta_ref[...] + jnp.dot(mix_ref[0:m, :], wout_ref[...],
                                     preferred_element_type=_F32)
        hm = _rmsnorm(xm, g2_ref[...]).astype(_BF16)
        for j in range(n_chunks):
            cf0_ref[j] = _ff_up_chunk(hm, wup_ref, j, d_ff)[m - hb:m, :]
        for c in range(ubuf.shape[0]):
            ca0_ref[c] = ubuf[c, m:m + ha, :]
        for c in range(zbuf.shape[0]):
            cb0_ref[c] = zbuf[c, m:m + hb, :]
        h2_ref[...] = jnp.zeros(h2_ref.shape, h2_ref.dtype)
        x1_ref[...] = jnp.zeros(x1_ref.shape, x1_ref.dtype)

    @pl.when(t_mix == 0)
    def _():
        for c in range(ubuf.shape[0]):
            ubuf[c, 0:ha, :] = ca0_ref[c]
        for c in range(zbuf.shape[0]):
            zbuf[c, 0:hb, :] = cb0_ref[c]

    @pl.when(jnp.logical_or(s == 0, t_ffn == 0))
    def _():
        ucar[...] = cf0_ref[...]

    def norm_rows_static(src_ref, g_ref, dst_ref, dtype):
        for r0 in range(0, tile, norm_rows):
            x = src_ref[r0:r0 + norm_rows, :]
            dst_ref[r0:r0 + norm_rows, :] = _rmsnorm(x, g_ref[...]).astype(dtype)

    @pl.when(has_mix)
    def _():
        norm_rows_static(x_ref, g1_ref, h1_ref, _BF16)
        proj_ref[...] = jnp.dot(h1_ref[...], win_ref[...], preferred_element_type=_F32)

    def mixer_chunk(i):
        rows = mix_rows
        mixer_rows(i * rows if isinstance(i, int) else pl.multiple_of(i * rows, rows), rows)

    upbufs = (upbuf0, upbuf1)

    def ffn_up(j, buf):
        _slab_store(buf, 0, ucar[j])
        _slab_store(buf, hb, _ff_up_chunk(h2_ref[...], wup_ref, j, d_ff))
        ucar[j] = _slab_load(buf, tile, hb)

    def ffn_act(j, buf):
        w = _ff_cols(wcf_ref, j, d_ff)
        for r0 in range(0, tile, ffn_rows):
            cv = (_rows_times(_slab_window(buf, r0 + hb - 2, ffn_rows), w[0])
                  + _rows_times(_slab_window(buf, r0 + hb - 1, ffn_rows), w[1])
                  + _rows_times(_slab_load(buf, r0 + hb, ffn_rows), w[2]))
            a = _silu(cv[:, 0:FF_CHUNK]) * cv[:, FF_CHUNK:2 * FF_CHUNK]
            act_ref[j, r0:r0 + ffn_rows, :] = a.astype(_BF16)

    def ffn_down(j, n):
        acc = x2_ref[...]
        for i in range(n):
            r0 = (j + i) * FF_CHUNK
            if not isinstance(j, int):
                r0 = pl.multiple_of(r0, FF_CHUNK)
            acc = acc + jnp.dot(act_ref[j + i], wdown_ref[pl.ds(r0, FF_CHUNK), :],
                                preferred_element_type=_F32)
        x2_ref[...] = acc

    n_mix = tile // mix_rows
    n_iter = (n_chunks - 3) // 2
    assert n_chunks >= 3 and mix_first + n_iter * mix_per_pair <= n_mix

    x2_ref[...] = x1_ref[...]
    ffn_up(0, upbuf0)
    ffn_up(1, upbuf1)
    ffn_act(0, upbuf0)
    for i in range(mix_first):
        mixer_chunk(i)

    def ffn_pair(k, c):
        j = 2 * k
        ffn_up(j + 2, upbuf0)
        ffn_act(j + 1, upbuf1)
        ffn_up(j + 3, upbuf1)
        ffn_act(j + 2, upbuf0)
        ffn_down(j, 2)
        for i in range(mix_per_pair):
            mixer_chunk(mix_first + k * mix_per_pair + i)
        return c
    lax.fori_loop(0, n_iter, ffn_pair, 0)
    j_done = 2 * n_iter
    for j in range(j_done + 2, n_chunks):
        ffn_up(j, upbufs[j % 2])
        ffn_act(j - 1, upbufs[(j - 1) % 2])
    ffn_act(n_chunks - 1, upbufs[(n_chunks - 1) % 2])
    ffn_down(j_done, n_chunks - j_done)

    for i in range(mix_first + n_iter * mix_per_pair, n_mix):
        mixer_chunk(i)
    x1_ref[...] = x_ref[...] + jnp.dot(mix_ref[...], wout_ref[...],
                                       preferred_element_type=_F32)
    norm_rows_static(x2_ref, gf_ref, y_ref, _F32)
    norm_rows_static(x1_ref, g2_ref, h2_ref, _BF16)
    _slab_store(ubuf, 0, _slab_load(ubuf, tile, ha))
    _slab_store(zbuf, 0, _slab_load(zbuf, tile, hb))

    @pl.when(jnp.logical_and(has_ffn, t_ffn == nt - 1))
    def _():
        for j in range(n_chunks):
            last = ucar[j, hb - 2:hb, :]
            for half in range(2):
                c0 = half * d_ff + j * FF_CHUNK
                cf_ref[:, c0:c0 + FF_CHUNK] = last[:, half * FF_CHUNK:(half + 1) * FF_CHUNK]

    @pl.when(jnp.logical_and(has_mix, t_mix == nt - 1))
    def _():
        ca_ref[...] = _slab_window(ubuf, off_a, conv_a_w - 1)
        cb_ref[...] = _slab_window(zbuf, hb - 2, 2)


def _prompt_layer(x, meta, wts, *, tile, norm_rows, mix_rows, mix_first, mix_per_pair,
                  ffn_rows):
    nb, length, d = x.shape
    nt = length // tile
    n_tiles = nb * nt
    (g1, win, wa, ba, lng, lnb, wb, beta_a, beta_b, wout, g2, wup, wcf, wdown, gf) = wts
    a_width = wa.shape[-1]
    b_width = wb.shape[-1]
    conv_a_w = wa.shape[0]
    d_ff = wdown.shape[0]
    n_chunks = d_ff // FF_CHUNK
    ha = -(-(conv_a_w - 1) // SUBLANES) * SUBLANES
    hb = SUBLANES
    assert meta.shape[0] % (2 * SUBLANES) == 0 and hb <= meta.shape[0] <= tile
    kern = functools.partial(_seq_kernel, tile=tile, nt=nt, n_tiles=n_tiles,
                             norm_rows=norm_rows, mix_rows=mix_rows,
                             mix_first=mix_first, mix_per_pair=mix_per_pair, ffn_rows=ffn_rows,
                             a_width=a_width, b_width=b_width, conv_a_w=conv_a_w,
                             d_ff=d_ff)
    mix_tile = lambda s: jnp.minimum(s, n_tiles - 1)
    ffn_tile = lambda s: jnp.maximum(s - 1, 0)
    const_spec = lambda a: pl.BlockSpec(a.shape, lambda s: (0,) * a.ndim,
                                        pipeline_mode=pl.Buffered(1))
    consts = (meta,) + tuple(wts)
    in_specs = [pl.BlockSpec((None, tile, d),
                             lambda s: (mix_tile(s) // nt, mix_tile(s) % nt, 0))]
    in_specs += [const_spec(c) for c in consts]
    state_shapes = ((conv_a_w - 1, a_width), (2, b_width), (2, 2 * d_ff))
    state_tiles = (mix_tile, mix_tile, ffn_tile)
    out_shape = (jax.ShapeDtypeStruct((nb, length, d), _F32),) + tuple(
        jax.ShapeDtypeStruct((nb,) + shp, _F32) for shp in state_shapes)
    out_specs = (
        pl.BlockSpec((None, tile, d), lambda s: (ffn_tile(s) // nt, ffn_tile(s) % nt, 0)),
    ) + tuple(
        pl.BlockSpec((None,) + shp, lambda s, f=f, n=len(shp): (f(s) // nt,) + (0,) * n)
        for shp, f in zip(state_shapes, state_tiles))
    scratch = [
        pltpu.VMEM((a_width // LANES, ha + tile, LANES), _F32),
        pltpu.VMEM((b_width // LANES, hb + tile, LANES), _F32),
        pltpu.VMEM((n_chunks, hb, 2 * FF_CHUNK), _F32),
        pltpu.VMEM((a_width // LANES, ha, LANES), _F32),
        pltpu.VMEM((b_width // LANES, hb, LANES), _F32),
        pltpu.VMEM((n_chunks, hb, 2 * FF_CHUNK), _F32),
        pltpu.VMEM((2 * FF_CHUNK // LANES, hb + tile, LANES), _F32),
        pltpu.VMEM((2 * FF_CHUNK // LANES, hb + tile, LANES), _F32),
        pltpu.VMEM((tile, d), _BF16),
        pltpu.VMEM((tile, d), _BF16),
        pltpu.VMEM((tile, win.shape[1]), _F32),
        pltpu.VMEM((tile, a_width + b_width), _BF16),
        pltpu.VMEM((tile, d), _F32),
        pltpu.VMEM((tile, d), _F32),
        pltpu.VMEM((n_chunks, tile, FF_CHUNK), _BF16),
    ]
    return pl.pallas_call(
        kern,
        grid=(n_tiles + 1,),
        in_specs=in_specs,
        out_specs=out_specs,
        out_shape=out_shape,
        scratch_shapes=scratch,
        compiler_params=pltpu.CompilerParams(
            dimension_semantics=("arbitrary",),
            vmem_limit_bytes=VMEM_LIMIT_BYTES),
        name="prompt_layer",
    )(x, *consts)


def _sample_kernel(xs_ref, sa_ref, sb_ref, sf_ref, g1_ref, win_ref, wa_ref, ba_ref,
                   lng_ref, lnb_ref, wb_ref, beta_a_ref, beta_b_ref, wout_ref, g2_ref,
                   wup_ref, wcf_ref, wdown_ref, gf_ref,
                   ys_ref, na_ref, nb_ref, nf_ref,
                   h_ref, proj_ref, u_ref, z_ref, mix_ref, x1_ref, up_ref, act_ref,
                   *, steps, d, a_width, b_width, conv_a_w, d_ff, n_chunks):
    s = xs_ref.shape[0]
    hist_a = conv_a_w - 1
    rows_of = lambda t: slice(t * s, (t + 1) * s)

    for t in range(steps):
        h_ref[rows_of(t), :] = _rmsnorm(xs_ref[:, t, :], g1_ref[...]).astype(_BF16)
    proj_ref[...] = jnp.dot(h_ref[...], win_ref[...], preferred_element_type=_F32)

    o = 2 * a_width
    for t in range(steps):
        r = rows_of(t)
        u_ref[r, :] = proj_ref[r, 0:a_width] * _sigmoid(proj_ref[r, a_width:o])
        z_ref[r, :] = (proj_ref[r, o + b_width:o + 2 * b_width]
                       * proj_ref[r, o + 2 * b_width:o + 3 * b_width])

    def xe_a(j):
        return sa_ref[j] if j < hist_a else u_ref[rows_of(j - hist_a), :]

    def xe_b(j):
        return sb_ref[:, j, :] if j < 2 else z_ref[rows_of(j - 2), :]

    for t in range(steps):
        r = rows_of(t)
        acc = jnp.broadcast_to(ba_ref[...], (s, a_width))
        for k in range(conv_a_w):
            acc = acc + _rows_times(xe_a(t + k), wa_ref[k])
        ya = _group_a_out(acc, lng_ref[...], lnb_ref[...], beta_a_ref[...])
        mix_ref[r, 0:a_width] = ya.astype(_BF16)
        zb = (_rows_times(xe_b(t), wb_ref[0]) + _rows_times(xe_b(t + 1), wb_ref[1])
              + _rows_times(xe_b(t + 2), wb_ref[2]))
        yb = proj_ref[r, o:o + b_width] * zb * beta_b_ref[...]
        mix_ref[r, a_width:a_width + b_width] = yb.astype(_BF16)

    for j in range(hist_a):
        na_ref[j] = xe_a(j + steps)
    for j in range(2):
        nb_ref[:, j, :] = xe_b(j + steps)

    x1_ref[...] = jnp.dot(mix_ref[...], wout_ref[...], preferred_element_type=_F32)
    for t in range(steps):
        r = rows_of(t)
        x1 = x1_ref[r, :] + xs_ref[:, t, :]
        x1_ref[r, :] = x1
        h_ref[r, :] = _rmsnorm(x1, g2_ref[...]).astype(_BF16)

    for j in range(n_chunks):
        up_ref[...] = _ff_up_chunk(h_ref[...], wup_ref, j, d_ff)
        w = _ff_cols(wcf_ref, j, d_ff)
        lo = j * FF_CHUNK

        def xe_f(i, half):
            if i < 2:
                c0 = half * d_ff + lo
                return sf_ref[:, i, c0:c0 + FF_CHUNK]
            return up_ref[rows_of(i - 2), half * FF_CHUNK:(half + 1) * FF_CHUNK]

        for t in range(steps):
            gv = []
            for half in range(2):
                wh = w[:, :, half * FF_CHUNK:(half + 1) * FF_CHUNK]
                gv.append(_rows_times(xe_f(t, half), wh[0]) + _rows_times(xe_f(t + 1, half), wh[1])
                          + _rows_times(xe_f(t + 2, half), wh[2]))
            act_ref[j, rows_of(t), :] = (_silu(gv[0]) * gv[1]).astype(_BF16)
        for i in range(2):
            for half in range(2):
                c0 = half * d_ff + lo
                nf_ref[:, i, c0:c0 + FF_CHUNK] = xe_f(i + steps, half)

    acc = _ff_down_chain(x1_ref[...], act_ref, wdown_ref, n_chunks)
    for t in range(steps):
        ys_ref[:, t, :] = _rmsnorm(acc[rows_of(t), :], gf_ref[...])


def _sample_layer(xs, sa, sb, sf, wts, *, steps, seq_block):
    n = xs.shape[0]
    (g1, win, wa, ba, lng, lnb, wb, beta_a, beta_b, wout, g2, wup, wcf, wdown, gf) = wts
    d = g1.shape[1]
    a_width = wa.shape[-1]
    b_width = wb.shape[-1]
    conv_a_w = wa.shape[0]
    d_ff = wdown.shape[0]
    n_chunks = d_ff // FF_CHUNK
    m = steps * seq_block
    kern = functools.partial(_sample_kernel, steps=steps, d=d, a_width=a_width,
                             b_width=b_width, conv_a_w=conv_a_w, d_ff=d_ff,
                             n_chunks=n_chunks)
    seq_spec = lambda a: pl.BlockSpec((seq_block,) + a.shape[1:], lambda i: (i, 0, 0))
    row_spec = lambda a: (pl.BlockSpec((a.shape[0], seq_block, a.shape[2]), lambda i: (0, i, 0))
                          if a is sa else seq_spec(a))
    const_spec = lambda a: pl.BlockSpec(a.shape, lambda i: (0,) * a.ndim,
                                        pipeline_mode=pl.Buffered(1))
    data = (xs, sa, sb, sf)
    in_specs = [row_spec(a) for a in data] + [const_spec(a) for a in wts]
    out_shape = tuple(jax.ShapeDtypeStruct(a.shape, _F32) for a in data)
    out_specs = tuple(row_spec(a) for a in data)
    scratch = [
        pltpu.VMEM((m, d), _BF16),
        pltpu.VMEM((m, win.shape[1]), _F32),
        pltpu.VMEM((m, a_width), _F32),
        pltpu.VMEM((m, b_width), _F32),
        pltpu.VMEM((m, a_width + b_width), _BF16),
        pltpu.VMEM((m, d), _F32),
        pltpu.VMEM((m, 2 * FF_CHUNK), _F32),
        pltpu.VMEM((n_chunks, m, FF_CHUNK), _BF16),
    ]
    return pl.pallas_call(
        kern,
        grid=(n // seq_block,),
        in_specs=in_specs,
        out_specs=out_specs,
        out_shape=out_shape,
        scratch_shapes=scratch,
        compiler_params=pltpu.CompilerParams(
            dimension_semantics=("arbitrary",),
            vmem_limit_bytes=VMEM_LIMIT_BYTES),
        name="sample_layer",
    )(*data, *wts)


def kernel(x_prompt, x_sample, state_conv_a, state_conv_b, state_conv_ffn, meta_tokens,
           norm_mix_g, w_in, w_conv_a, b_conv_a, gn_a_g, gn_a_b, w_conv_b, beta_a, beta_b,
           w_out, norm_ffn_g, w_up, w_conv_f, w_down, norm_final_g):
    depth = w_in.shape[0]
    assert depth == 1, "single-layer step only"
    n_dec, dec_seq, _ = x_sample.shape
    a_width = w_conv_a.shape[2]
    d_ff = w_down.shape[1]
    assert x_prompt.shape[1] % PROMPT_TILING["tile"] == 0 and n_dec % SAMPLE_SEQ_BLOCK == 0
    assert a_width % LANES == 0 and LANES == 2 * HEAD_DIM
    assert d_ff % FF_CHUNK == 0 and w_conv_b.shape[1] == 3 and w_conv_f.shape[1] == 3

    row = lambda v: v.reshape(1, -1)
    tile8 = lambda w: jnp.broadcast_to(w[:, None, :], (w.shape[0], SUBLANES, w.shape[1]))
    wts = (
        row(norm_mix_g[0]),
        w_in[0].astype(_BF16),
        tile8(w_conv_a[0]), row(b_conv_a[0]), row(gn_a_g[0]), row(gn_a_b[0]),
        tile8(w_conv_b[0]), row(beta_a[0]), row(beta_b[0]),
        w_out[0].astype(_BF16),
        row(norm_ffn_g[0]),
        w_up[0].astype(_BF16),
        tile8(w_conv_f[0]),
        w_down[0].astype(_BF16),
        row(norm_final_g),
    )

    y_prompt, na_p, nb_p, nf_p = _prompt_layer(x_prompt, meta_tokens, wts, **PROMPT_TILING)

    y_s, na_s, nb_s, nf_s = _sample_layer(
        x_sample, jnp.transpose(state_conv_a[0], (1, 0, 2)), state_conv_b[0],
        state_conv_ffn[0], wts, steps=dec_seq, seq_block=SAMPLE_SEQ_BLOCK)
    na_s = jnp.transpose(na_s, (1, 0, 2))

    return (y_prompt, y_s, na_p[None], nb_p[None], nf_p[None], na_s[None], nb_s[None],
            nf_s[None])
```

```python
import functools

import jax
import jax.numpy as jnp
from jax import lax
from jax.experimental import pallas as pl
from jax.experimental.pallas import tpu as pltpu

EPS = 1e-6
HEAD_DIM = 64
LANES = 128
SUBLANES = 8
FF_CHUNK = 256
VMEM_LIMIT_BYTES = 56 * 1024 * 1024

PROMPT_TILING = dict(tile=512, norm_rows=64, mix_rows=32, mix_first=5, mix_per_pair=2,
                     ffn_rows=32)
SAMPLE_SEQ_BLOCK = 32

_F32 = jnp.float32
_BF16 = jnp.bfloat16


def _rmsnorm(x, g):
    ms = jnp.mean(x * x, axis=-1, keepdims=True)
    return x * lax.rsqrt(ms + EPS) * g


def _sigmoid(x):
    return 1.0 / (1.0 + jnp.exp(-x))


def _silu(x):
    return x * _sigmoid(x)


def _group_mean(v, lane_lo):
    sa = jnp.sum(jnp.where(lane_lo, v, 0.0), axis=1, keepdims=True)
    sb = jnp.sum(jnp.where(lane_lo, 0.0, v), axis=1, keepdims=True)
    return jnp.where(lane_lo, sa, sb) * (1.0 / HEAD_DIM)


def _head_layernorm(ua):
    rows, width = ua.shape
    lane_lo = lax.broadcasted_iota(jnp.int32, (rows, LANES), 1) < HEAD_DIM
    outs = []
    for j in range(width // LANES):
        xb = ua[:, j * LANES:(j + 1) * LANES]
        d = xb - _group_mean(xb, lane_lo)
        var = _group_mean(d * d, lane_lo)
        outs.append(d * lax.rsqrt(var + EPS))
    return jnp.concatenate(outs, axis=1)


def _group_a_out(ua, lng, lnb, beta_a):
    return _silu(_head_layernorm(ua) * lng + lnb) * beta_a


def _rows_times(x, w8):
    r, c = x.shape
    return (x.reshape(r // SUBLANES, SUBLANES, c) * w8).reshape(r, c)


def _ff_cols(ref, j, d_ff):
    off = j * FF_CHUNK
    if not isinstance(j, int):
        off = pl.multiple_of(off, FF_CHUNK)
    return jnp.concatenate([ref[:, :, pl.ds(off, FF_CHUNK)],
                            ref[:, :, pl.ds(d_ff + off, FF_CHUNK)]], axis=2)


def _ff_up_chunk(h, wup_ref, j, d_ff):
    off = j * FF_CHUNK
    if not isinstance(j, int):
        off = pl.multiple_of(off, FF_CHUNK)
    g = jnp.dot(h, wup_ref[:, pl.ds(off, FF_CHUNK)], preferred_element_type=_F32)
    v = jnp.dot(h, wup_ref[:, pl.ds(d_ff + off, FF_CHUNK)], preferred_element_type=_F32)
    return jnp.concatenate([g, v], axis=1)


def _ff_down_chain(acc, act_ref, wdown_ref, n_chunks):
    for j in range(n_chunks):
        acc = acc + jnp.dot(act_ref[j], wdown_ref[j * FF_CHUNK:(j + 1) * FF_CHUNK, :],
                            preferred_element_type=_F32)
    return acc


def _slab_store(buf, row0, val):
    for c in range(buf.shape[0]):
        buf[c, pl.ds(row0, val.shape[0]), :] = val[:, c * LANES:(c + 1) * LANES]


def _slab_load(buf, row0, nrows):
    return jnp.concatenate(
        [buf[c, pl.ds(row0, nrows), :] for c in range(buf.shape[0])], axis=1)


def _slab_window(buf, row0, nrows):
    return jnp.concatenate(
        [buf[c, pl.ds(row0, nrows, stride=1), :] for c in range(buf.shape[0])], axis=1)


def _seq_kernel(x_ref, meta_ref, g1_ref, win_ref, wa_ref, ba_ref,
                lng_ref, lnb_ref, wb_ref, beta_a_ref, beta_b_ref, wout_ref, g2_ref,
                wup_ref, wcf_ref, wdown_ref, gf_ref,
                y_ref, ca_ref, cb_ref, cf_ref,
                ubuf, zbuf, ucar, ca0_ref, cb0_ref, cf0_ref, upbuf0, upbuf1, h1_ref, h2_ref,
                proj_ref, mix_ref, x2_ref, act_ref,
                *, tile, nt, n_tiles, norm_rows, mix_rows, mix_first, mix_per_pair, ffn_rows,
                a_width, b_width, conv_a_w, d_ff):
    n_chunks = d_ff // FF_CHUNK
    s = pl.program_id(0)
    t_mix = lax.rem(s, nt)
    t_ffn = lax.rem(s + nt - 1, nt)
    has_mix = s < n_tiles
    has_ffn = s > 0
    ha = ubuf.shape[1] - tile
    hb = zbuf.shape[1] - tile
    off_a = ha - (conv_a_w - 1)

    def mixer_rows(r0, rows):
        p = lambda lo, w: proj_ref[pl.ds(r0, rows), lo:lo + w]
        u = p(0, a_width) * _sigmoid(p(a_width, a_width))
        _slab_store(ubuf, ha + r0, u)
        acc = jnp.broadcast_to(ba_ref[...], (rows, a_width))
        for k in range(conv_a_w):
            acc = acc + _rows_times(_slab_window(ubuf, r0 + off_a + k, rows), wa_ref[k])
        ya = _group_a_out(acc, lng_ref[...], lnb_ref[...], beta_a_ref[...])
        mix_ref[pl.ds(r0, rows), 0:a_width] = ya.astype(_BF16)

        o = 2 * a_width
        z = p(o + b_width, b_width) * p(o + 2 * b_width, b_width)
        _slab_store(zbuf, hb + r0, z)
        zb = (_rows_times(_slab_window(zbuf, r0 + hb - 2, rows), wb_ref[0])
              + _rows_times(_slab_window(zbuf, r0 + hb - 1, rows), wb_ref[1])
              + _rows_times(z, wb_ref[2]))
        yb = p(o, b_width) * zb * beta_b_ref[...]
        mix_ref[pl.ds(r0, rows), a_width:a_width + b_width] = yb.astype(_BF16)

    @pl.when(s == 0)
    def _():
        m = meta_ref.shape[0]
        _slab_store(ubuf, 0, jnp.zeros((ha, a_width), _F32))
        _slab_store(zbuf, 0, jnp.zeros((hb, b_width), _F32))
        h1_ref[0:m, :] = _rmsnorm(meta_ref[...], g1_ref[...]).astype(_BF16)
        proj_ref[0:m, :] = jnp.dot(h1_ref[0:m, :], win_ref[...], preferred_element_type=_F32)
        mixer_rows(0, m)
        xm = meta_ref[...] + jnp.dot(mix_ref[0:m, :], wout_ref[...],
                                     preferred_element_type=_F32)
        hm = _rmsnorm(xm, g2_ref[...]).astype(_BF16)
        for j in range(n_chunks):
            cf0_ref[j] = _ff_up_chunk(hm, wup_ref, j, d_ff)[m - hb:m, :]
        for c in range(ubuf.shape[0]):
            ca0_ref[c] = ubuf[c, m:m + ha, :]
        for c in range(zbuf.shape[0]):
            cb0_ref[c] = zbuf[c, m:m + hb, :]
        h2_ref[...] = jnp.zeros(h2_ref.shape, h2_ref.dtype)
        x2_ref[...] = jnp.zeros(x2_ref.shape, x2_ref.dtype)

    @pl.when(t_mix == 0)
    def _():
        for c in range(ubuf.shape[0]):
            ubuf[c, 0:ha, :] = ca0_ref[c]
        for c in range(zbuf.shape[0]):
            zbuf[c, 0:hb, :] = cb0_ref[c]

    @pl.when(jnp.logical_or(s == 0, t_ffn == 0))
    def _():
        ucar[...] = cf0_ref[...]

    def norm_rows_static(src_ref, g_ref, dst_ref, dtype):
        for r0 in range(0, tile, norm_rows):
            x = src_ref[r0:r0 + norm_rows, :]
            dst_ref[r0:r0 + norm_rows, :] = _rmsnorm(x, g_ref[...]).astype(dtype)

    @pl.when(has_mix)
    def _():
        norm_rows_static(x_ref, g1_ref, h1_ref, _BF16)
        proj_ref[...] = jnp.dot(h1_ref[...], win_ref[...], preferred_element_type=_F32)

    def mixer_chunk(i):
        rows = mix_rows
        mixer_rows(i * rows if isinstance(i, int) else pl.multiple_of(i * rows, rows), rows)

    upbufs = (upbuf0, upbuf1)

    def ffn_up(j, buf):
        _slab_store(buf, 0, ucar[j])
        _slab_store(buf, hb, _ff_up_chunk(h2_ref[...], wup_ref, j, d_ff))
        ucar[j] = _slab_load(buf, tile, hb)

    def ffn_act(j, buf):
        w = _ff_cols(wcf_ref, j, d_ff)
        for r0 in range(0, tile, ffn_rows):
            cv = (_rows_times(_slab_window(buf, r0 + hb - 2, ffn_rows), w[0])
                  + _rows_times(_slab_window(buf, r0 + hb - 1, ffn_rows), w[1])
                  + _rows_times(_slab_load(buf, r0 + hb, ffn_rows), w[2]))
            a = _silu(cv[:, 0:FF_CHUNK]) * cv[:, FF_CHUNK:2 * FF_CHUNK]
            act_ref[j, r0:r0 + ffn_rows, :] = a.astype(_BF16)

    def ffn_down(j, n):
        acc = x2_ref[...]
        for i in range(n):
            r0 = (j + i) * FF_CHUNK
            if not isinstance(j, int):
                r0 = pl.multiple_of(r0, FF_CHUNK)
            acc = acc + jnp.dot(act_ref[j + i], wdown_ref[pl.ds(r0, FF_CHUNK), :],
                                preferred_element_type=_F32)
        x2_ref[...] = acc

    n_mix = tile // mix_rows
    n_iter = (n_chunks - 3) // 2
    assert n_chunks >= 3 and mix_first + n_iter * mix_per_pair <= n_mix

    ffn_up(0, upbuf0)
    ffn_up(1, upbuf1)
    ffn_act(0, upbuf0)
    for i in range(mix_first):
        mixer_chunk(i)

    def ffn_pair(k, c):
        j = 2 * k
        ffn_up(j + 2, upbuf0)
        ffn_act(j + 1, upbuf1)
        ffn_up(j + 3, upbuf1)
        ffn_act(j + 2, upbuf0)
        ffn_down(j, 2)
        for i in range(mix_per_pair):
            mixer_chunk(mix_first + k * mix_per_pair + i)
        return c
    lax.fori_loop(0, n_iter, ffn_pair, 0)
    j_done = 2 * n_iter
    for j in range(j_done + 2, n_chunks):
        ffn_up(j, upbufs[j % 2])
        ffn_act(j - 1, upbufs[(j - 1) % 2])
    ffn_act(n_chunks - 1, upbufs[(n_chunks - 1) % 2])
    ffn_down(j_done, n_chunks - j_done)

    for i in range(mix_first + n_iter * mix_per_pair, n_mix):
        mixer_chunk(i)
    norm_rows_static(x2_ref, gf_ref, y_ref, _F32)
    x2_ref[...] = x_ref[...] + jnp.dot(mix_ref[...], wout_ref[...],
                                       preferred_element_type=_F32)
    norm_rows_static(x2_ref, g2_ref, h2_ref, _BF16)
    _slab_store(ubuf, 0, _slab_load(ubuf, tile, ha))
    _slab_store(zbuf, 0, _slab_load(zbuf, tile, hb))

    @pl.when(jnp.logical_and(has_ffn, t_ffn == nt - 1))
    def _():
        for j in range(n_chunks):
            last = ucar[j, hb - 2:hb, :]
            for half in range(2):
                c0 = half * d_ff + j * FF_CHUNK
                cf_ref[:, c0:c0 + FF_CHUNK] = last[:, half * FF_CHUNK:(half + 1) * FF_CHUNK]

    @pl.when(jnp.logical_and(has_mix, t_mix == nt - 1))
    def _():
        ca_ref[...] = _slab_window(ubuf, off_a, conv_a_w - 1)
        cb_ref[...] = _slab_window(zbuf, hb - 2, 2)


def _prompt_layer(x, meta, wts, *, tile, norm_rows, mix_rows, mix_first, mix_per_pair,
                  ffn_rows):
    nb, length, d = x.shape
    nt = length // tile
    n_tiles = nb * nt
    (g1, win, wa, ba, lng, lnb, wb, beta_a, beta_b, wout, g2, wup, wcf, wdown, gf) = wts
    a_width = wa.shape[-1]
    b_width = wb.shape[-1]
    conv_a_w = wa.shape[0]
    d_ff = wdown.shape[0]
    n_chunks = d_ff // FF_CHUNK
    ha = -(-(conv_a_w - 1) // SUBLANES) * SUBLANES
    hb = SUBLANES
    assert meta.shape[0] % (2 * SUBLANES) == 0 and hb <= meta.shape[0] <= tile
    kern = functools.partial(_seq_kernel, tile=tile, nt=nt, n_tiles=n_tiles,
                             norm_rows=norm_rows, mix_rows=mix_rows,
                             mix_first=mix_first, mix_per_pair=mix_per_pair, ffn_rows=ffn_rows,
                             a_width=a_width, b_width=b_width, conv_a_w=conv_a_w,
                             d_ff=d_ff)
    mix_tile = lambda s: jnp.minimum(s, n_tiles - 1)
    ffn_tile = lambda s: jnp.maximum(s - 1, 0)
    const_spec = lambda a: pl.BlockSpec(a.shape, lambda s: (0,) * a.ndim,
                                        pipeline_mode=pl.Buffered(1))
    consts = (meta,) + tuple(wts)
    in_specs = [pl.BlockSpec((None, tile, d),
                             lambda s: (mix_tile(s) // nt, mix_tile(s) % nt, 0))]
    in_specs += [const_spec(c) for c in consts]
    state_shapes = ((conv_a_w - 1, a_width), (2, b_width), (2, 2 * d_ff))
    state_tiles = (mix_tile, mix_tile, ffn_tile)
    out_shape = (jax.ShapeDtypeStruct((nb, length, d), _F32),) + tuple(
        jax.ShapeDtypeStruct((nb,) + shp, _F32) for shp in state_shapes)
    out_specs = (
        pl.BlockSpec((None, tile, d), lambda s: (ffn_tile(s) // nt, ffn_tile(s) % nt, 0)),
    ) + tuple(
        pl.BlockSpec((None,) + shp, lambda s, f=f, n=len(shp): (f(s) // nt,) + (0,) * n)
        for shp, f in zip(state_shapes, state_tiles))
    scratch = [
        pltpu.VMEM((a_width // LANES, ha + tile, LANES), _F32),
        pltpu.VMEM((b_width // LANES, hb + tile, LANES), _F32),
        pltpu.VMEM((n_chunks, hb, 2 * FF_CHUNK), _F32),
        pltpu.VMEM((a_width // LANES, ha, LANES), _F32),
        pltpu.VMEM((b_width // LANES, hb, LANES), _F32),
        pltpu.VMEM((n_chunks, hb, 2 * FF_CHUNK), _F32),
        pltpu.VMEM((2 * FF_CHUNK // LANES, hb + tile, LANES), _F32),
        pltpu.VMEM((2 * FF_CHUNK // LANES, hb + tile, LANES), _F32),
        pltpu.VMEM((tile, d), _BF16),
        pltpu.VMEM((tile, d), _BF16),
        pltpu.VMEM((tile, win.shape[1]), _F32),
        pltpu.VMEM((tile, a_width + b_width), _BF16),
        pltpu.VMEM((tile, d), _F32),
        pltpu.VMEM((n_chunks, tile, FF_CHUNK), _BF16),
    ]
    return pl.pallas_call(
        kern,
        grid=(n_tiles + 1,),
        in_specs=in_specs,
        out_specs=out_specs,
        out_shape=out_shape,
        scratch_shapes=scratch,
        compiler_params=pltpu.CompilerParams(
            dimension_semantics=("arbitrary",),
            vmem_limit_bytes=VMEM_LIMIT_BYTES),
        name="prompt_layer",
    )(x, *consts)


def _sample_kernel(xs_ref, sa_ref, sb_ref, sf_ref, g1_ref, win_ref, wa_ref, ba_ref,
                   lng_ref, lnb_ref, wb_ref, beta_a_ref, beta_b_ref, wout_ref, g2_ref,
                   wup_ref, wcf_ref, wdown_ref, gf_ref,
                   ys_ref, na_ref, nb_ref, nf_ref,
                   h_ref, proj_ref, u_ref, z_ref, mix_ref, x1_ref, up_ref, act_ref, xt_ref,
                   *, steps, d, a_width, b_width, conv_a_w, d_ff, n_chunks):
    s = xs_ref.shape[0]
    hist_a = conv_a_w - 1
    rows_of = lambda t: slice(t * s, (t + 1) * s)

    for t in range(steps):
        xt_ref[t] = xs_ref[:, t, :]
    for t in range(steps):
        h_ref[rows_of(t), :] = _rmsnorm(xt_ref[t], g1_ref[...]).astype(_BF16)
    proj_ref[...] = jnp.dot(h_ref[...], win_ref[...], preferred_element_type=_F32)

    o = 2 * a_width
    for t in range(steps):
        r = rows_of(t)
        u_ref[r, :] = proj_ref[r, 0:a_width] * _sigmoid(proj_ref[r, a_width:o])
        z_ref[r, :] = (proj_ref[r, o + b_width:o + 2 * b_width]
                       * proj_ref[r, o + 2 * b_width:o + 3 * b_width])

    def xe_a(j):
        return sa_ref[j] if j < hist_a else u_ref[rows_of(j - hist_a), :]

    def xe_b(j):
        return sb_ref[:, j, :] if j < 2 else z_ref[rows_of(j - 2), :]

    for t in range(steps):
        r = rows_of(t)
        acc = jnp.broadcast_to(ba_ref[...], (s, a_width))
        for k in range(conv_a_w):
            acc = acc + _rows_times(xe_a(t + k), wa_ref[k])
        ya = _group_a_out(acc, lng_ref[...], lnb_ref[...], beta_a_ref[...])
        mix_ref[r, 0:a_width] = ya.astype(_BF16)
        zb = (_rows_times(xe_b(t), wb_ref[0]) + _rows_times(xe_b(t + 1), wb_ref[1])
              + _rows_times(xe_b(t + 2), wb_ref[2]))
        yb = proj_ref[r, o:o + b_width] * zb * beta_b_ref[...]
        mix_ref[r, a_width:a_width + b_width] = yb.astype(_BF16)

    for j in range(hist_a):
        na_ref[j] = xe_a(j + steps)
    for j in range(2):
        nb_ref[:, j, :] = xe_b(j + steps)

    x1_ref[...] = jnp.dot(mix_ref[...], wout_ref[...], preferred_element_type=_F32)
    for t in range(steps):
        r = rows_of(t)
        x1 = x1_ref[r, :] + xt_ref[t]
        x1_ref[r, :] = x1
        h_ref[r, :] = _rmsnorm(x1, g2_ref[...]).astype(_BF16)

    for j in range(n_chunks):
        up_ref[...] = _ff_up_chunk(h_ref[...], wup_ref, j, d_ff)
        w = _ff_cols(wcf_ref, j, d_ff)
        lo = j * FF_CHUNK

        def xe_f(i, half):
            if i < 2:
                c0 = half * d_ff + lo
                return sf_ref[:, i, c0:c0 + FF_CHUNK]
            return up_ref[rows_of(i - 2), half * FF_CHUNK:(half + 1) * FF_CHUNK]

        for t in range(steps):
            gv = []
            for half in range(2):
                wh = w[:, :, half * FF_CHUNK:(half + 1) * FF_CHUNK]
                gv.append(_rows_times(xe_f(t, half), wh[0]) + _rows_times(xe_f(t + 1, half), wh[1])
                          + _rows_times(xe_f(t + 2, half), wh[2]))
            act_ref[j, rows_of(t), :] = (_silu(gv[0]) * gv[1]).astype(_BF16)
        for i in range(2):
            for half in range(2):
                c0 = half * d_ff + lo
                nf_ref[:, i, c0:c0 + FF_CHUNK] = xe_f(i + steps, half)

    acc = _ff_down_chain(x1_ref[...], act_ref, wdown_ref, n_chunks)
    for t in range(steps):
        ys_ref[:, t, :] = _rmsnorm(acc[rows_of(t), :], gf_ref[...])


def _sample_layer(xs, sa, sb, sf, wts, *, steps, seq_block):
    n = xs.shape[0]
    (g1, win, wa, ba, lng, lnb, wb, beta_a, beta_b, wout, g2, wup, wcf, wdown, gf) = wts
    d = g1.shape[1]
    a_width = wa.shape[-1]
    b_width = wb.shape[-1]
    conv_a_w = wa.shape[0]
    d_ff = wdown.shape[0]
    n_chunks = d_ff // FF_CHUNK
    m = steps * seq_block
    kern = functools.partial(_sample_kernel, steps=steps, d=d, a_width=a_width,
                             b_width=b_width, conv_a_w=conv_a_w, d_ff=d_ff,
                             n_chunks=n_chunks)
    seq_spec = lambda a: pl.BlockSpec((seq_block,) + a.shape[1:], lambda i: (i, 0, 0))
    row_spec = lambda a: (pl.BlockSpec((a.shape[0], seq_block, a.shape[2]), lambda i: (0, i, 0))
                          if a is sa else seq_spec(a))
    const_spec = lambda a: pl.BlockSpec(a.shape, lambda i: (0,) * a.ndim,
                                        pipeline_mode=pl.Buffered(1))
    data = (xs, sa, sb, sf)
    in_specs = [row_spec(a) for a in data] + [const_spec(a) for a in wts]
    out_shape = tuple(jax.ShapeDtypeStruct(a.shape, _F32) for a in data)
    out_specs = tuple(row_spec(a) for a in data)
    scratch = [
        pltpu.VMEM((m, d), _BF16),
        pltpu.VMEM((m, win.shape[1]), _F32),
        pltpu.VMEM((m, a_width), _F32),
        pltpu.VMEM((m, b_width), _F32),
        pltpu.VMEM((m, a_width + b_width), _BF16),
        pltpu.VMEM((m, d), _F32),
        pltpu.VMEM((m, 2 * FF_CHUNK), _F32),
        pltpu.VMEM((n_chunks, m, FF_CHUNK), _BF16),
        pltpu.VMEM((steps, seq_block, d), _F32),
    ]
    return pl.pallas_call(
        kern,
        grid=(n // seq_block,),
        in_specs=in_specs,
        out_specs=out_specs,
        out_shape=out_shape,
        scratch_shapes=scratch,
        compiler_params=pltpu.CompilerParams(
            dimension_semantics=("arbitrary",),
            vmem_limit_bytes=VMEM_LIMIT_BYTES),
        name="sample_layer",
    )(*data, *wts)


def kernel(x_prompt, x_sample, state_conv_a, state_conv_b, state_conv_ffn, meta_tokens,
           norm_mix_g, w_in, w_conv_a, b_conv_a, gn_a_g, gn_a_b, w_conv_b, beta_a, beta_b,
           w_out, norm_ffn_g, w_up, w_conv_f, w_down, norm_final_g):
    depth = w_in.shape[0]
    assert depth == 1, "single-layer step only"
    n_dec, dec_seq, _ = x_sample.shape
    a_width = w_conv_a.shape[2]
    d_ff = w_down.shape[1]
    assert x_prompt.shape[1] % PROMPT_TILING["tile"] == 0 and n_dec % SAMPLE_SEQ_BLOCK == 0
    assert a_width % LANES == 0 and LANES == 2 * HEAD_DIM
    assert d_ff % FF_CHUNK == 0 and w_conv_b.shape[1] == 3 and w_conv_f.shape[1] == 3

    row = lambda v: v.reshape(1, -1)
    tile8 = lambda w: jnp.broadcast_to(w[:, None, :], (w.shape[0], SUBLANES, w.shape[1]))
    wts = (
        row(norm_mix_g[0]),
        w_in[0].astype(_BF16),
        tile8(w_conv_a[0]), row(b_conv_a[0]), row(gn_a_g[0]), row(gn_a_b[0]),
        tile8(w_conv_b[0]), row(beta_a[0]), row(beta_b[0]),
        w_out[0].astype(_BF16),
        row(norm_ffn_g[0]),
        w_up[0].astype(_BF16),
        tile8(w_conv_f[0]),
        w_down[0].astype(_BF16),
        row(norm_final_g),
    )

    y_prompt, na_p, nb_p, nf_p = _prompt_layer(x_prompt, meta_tokens, wts, **PROMPT_TILING)

    y_s, na_s, nb_s, nf_s = _sample_layer(
        x_sample, jnp.transpose(state_conv_a[0], (1, 0, 2)), state_conv_b[0],
        state_conv_ffn[0], wts, steps=dec_seq, seq_block=SAMPLE_SEQ_BLOCK)
    na_s = jnp.transpose(na_s, (1, 0, 2))

    return (y_prompt, y_s, na_p[None], nb_p[None], nf_p[None], na_s[None], nb_s[None],
            nf_s[None])
```

```python
import functools

import jax
import jax.numpy as jnp
from jax import lax
from jax.experimental import pallas as pl
from jax.experimental.pallas import tpu as pltpu

EPS = 1e-6
HEAD_DIM = 64
LANES = 128
SUBLANES = 8
FF_CHUNK = 256
VMEM_LIMIT_BYTES = 56 * 1024 * 1024

PROMPT_TILING = dict(tile=512, norm_rows=64, mix_rows=32, mix_first=5, mix_per_pair=2,
                     ffn_rows=32)
SAMPLE_SEQ_BLOCK = 32

_F32 = jnp.float32
_BF16 = jnp.bfloat16


def _rmsnorm(x, g):
    ms = jnp.mean(x * x, axis=-1, keepdims=True)
    return x * lax.rsqrt(ms + EPS) * g


def _sigmoid(x):
    return 1.0 / (1.0 + jnp.exp(-x))


def _silu(x):
    return x * _sigmoid(x)


def _group_mean(v, lane_lo):
    sa = jnp.sum(jnp.where(lane_lo, v, 0.0), axis=1, keepdims=True)
    sb = jnp.sum(jnp.where(lane_lo, 0.0, v), axis=1, keepdims=True)
    return jnp.where(lane_lo, sa, sb) * (1.0 / HEAD_DIM)


def _head_layernorm(ua):
    rows, width = ua.shape
    lane_lo = lax.broadcasted_iota(jnp.int32, (rows, LANES), 1) < HEAD_DIM
    outs = []
    for j in range(width // LANES):
        xb = ua[:, j * LANES:(j + 1) * LANES]
        d = xb - _group_mean(xb, lane_lo)
        var = _group_mean(d * d, lane_lo)
        outs.append(d * lax.rsqrt(var + EPS))
    return jnp.concatenate(outs, axis=1)


def _group_a_out(ua, lng, lnb, beta_a):
    return _silu(_head_layernorm(ua) * lng + lnb) * beta_a


def _rows_times(x, w8):
    r, c = x.shape
    return (x.reshape(r // SUBLANES, SUBLANES, c) * w8).reshape(r, c)


def _ff_cols(ref, j, d_ff):
    off = j * FF_CHUNK
    if not isinstance(j, int):
        off = pl.multiple_of(off, FF_CHUNK)
    return jnp.concatenate([ref[:, :, pl.ds(off, FF_CHUNK)],
                            ref[:, :, pl.ds(d_ff + off, FF_CHUNK)]], axis=2)


def _ff_up_chunk(h, wup_ref, j, d_ff):
    off = j * FF_CHUNK
    if not isinstance(j, int):
        off = pl.multiple_of(off, FF_CHUNK)
    g = jnp.dot(h, wup_ref[:, pl.ds(off, FF_CHUNK)], preferred_element_type=_F32)
    v = jnp.dot(h, wup_ref[:, pl.ds(d_ff + off, FF_CHUNK)], preferred_element_type=_F32)
    return jnp.concatenate([g, v], axis=1)


def _ff_down_chain(acc, act_ref, wdown_ref, n_chunks):
    for j in range(n_chunks):
        acc = acc + jnp.dot(act_ref[j], wdown_ref[j * FF_CHUNK:(j + 1) * FF_CHUNK, :],
                            preferred_element_type=_F32)
    return acc


def _slab_store(buf, row0, val):
    for c in range(buf.shape[0]):
        buf[c, pl.ds(row0, val.shape[0]), :] = val[:, c * LANES:(c + 1) * LANES]


def _slab_load(buf, row0, nrows):
    return jnp.concatenate(
        [buf[c, pl.ds(row0, nrows), :] for c in range(buf.shape[0])], axis=1)


def _slab_window(buf, row0, nrows):
    return jnp.concatenate(
        [buf[c, pl.ds(row0, nrows, stride=1), :] for c in range(buf.shape[0])], axis=1)


def _seq_kernel(x_ref, meta_ref, g1_ref, win_ref, wa_ref, ba_ref,
                lng_ref, lnb_ref, wb_ref, beta_a_ref, beta_b_ref, wout_ref, g2_ref,
                wup_ref, wcf_ref, wdown_ref, gf_ref,
                y_ref, ca_ref, cb_ref, cf_ref,
                ubuf, zbuf, ucar, ca0_ref, cb0_ref, cf0_ref, upbuf0, upbuf1, h1_ref, h2_ref,
                proj_ref, mix_ref, x2_ref, act_ref,
                *, tile, nt, n_tiles, norm_rows, mix_rows, mix_first, mix_per_pair, ffn_rows,
                a_width, b_width, conv_a_w, d_ff):
    n_chunks = d_ff // FF_CHUNK
    s = pl.program_id(0)
    t_mix = lax.rem(s, nt)
    t_ffn = lax.rem(s + nt - 1, nt)
    has_mix = s < n_tiles
    has_ffn = s > 0
    ha = ubuf.shape[1] - tile
    hb = zbuf.shape[1] - tile
    off_a = ha - (conv_a_w - 1)

    def mixer_rows(r0, rows):
        p = lambda lo, w: proj_ref[pl.ds(r0, rows), lo:lo + w]
        u = p(0, a_width) * _sigmoid(p(a_width, a_width))
        _slab_store(ubuf, ha + r0, u)
        acc = jnp.broadcast_to(ba_ref[...], (rows, a_width))
        for k in range(conv_a_w):
            acc = acc + _rows_times(_slab_window(ubuf, r0 + off_a + k, rows), wa_ref[k])
        ya = _group_a_out(acc, lng_ref[...], lnb_ref[...], beta_a_ref[...])
        mix_ref[pl.ds(r0, rows), 0:a_width] = ya.astype(_BF16)

        o = 2 * a_width
        z = p(o + b_width, b_width) * p(o + 2 * b_width, b_width)
        _slab_store(zbuf, hb + r0, z)
        zb = (_rows_times(_slab_window(zbuf, r0 + hb - 2, rows), wb_ref[0])
              + _rows_times(_slab_window(zbuf, r0 + hb - 1, rows), wb_ref[1])
              + _rows_times(z, wb_ref[2]))
        yb = p(o, b_width) * zb * beta_b_ref[...]
        mix_ref[pl.ds(r0, rows), a_width:a_width + b_width] = yb.astype(_BF16)

    @pl.when(s == 0)
    def _():
        m = meta_ref.shape[0]
        _slab_store(ubuf, 0, jnp.zeros((ha, a_width), _F32))
        _slab_store(zbuf, 0, jnp.zeros((hb, b_width), _F32))
        h1_ref[0:m, :] = _rmsnorm(meta_ref[...], g1_ref[...]).astype(_BF16)
        proj_ref[0:m, :] = jnp.dot(h1_ref[0:m, :], win_ref[...], preferred_element_type=_F32)
        mixer_rows(0, m)
        xm = meta_ref[...] + jnp.dot(mix_ref[0:m, :], wout_ref[...],
                                     preferred_element_type=_F32)
        hm = _rmsnorm(xm, g2_ref[...]).astype(_BF16)
        for j in range(n_chunks):
            cf0_ref[j] = _ff_up_chunk(hm, wup_ref, j, d_ff)[m - hb:m, :]
        for c in range(ubuf.shape[0]):
            ca0_ref[c] = ubuf[c, m:m + ha, :]
        for c in range(zbuf.shape[0]):
            cb0_ref[c] = zbuf[c, m:m + hb, :]

    @pl.when(t_mix == 0)
    def _():
        for c in range(ubuf.shape[0]):
            ubuf[c, 0:ha, :] = ca0_ref[c]
        for c in range(zbuf.shape[0]):
            zbuf[c, 0:hb, :] = cb0_ref[c]

    @pl.when(jnp.logical_and(has_ffn, t_ffn == 0))
    def _():
        ucar[...] = cf0_ref[...]

    def norm_rows_static(src_ref, g_ref, dst_ref, dtype):
        for r0 in range(0, tile, norm_rows):
            x = src_ref[r0:r0 + norm_rows, :]
            dst_ref[r0:r0 + norm_rows, :] = _rmsnorm(x, g_ref[...]).astype(dtype)

    @pl.when(has_mix)
    def _():
        norm_rows_static(x_ref, g1_ref, h1_ref, _BF16)
        proj_ref[...] = jnp.dot(h1_ref[...], win_ref[...], preferred_element_type=_F32)

    def mixer_chunk(i):
        rows = mix_rows
        mixer_rows(i * rows if isinstance(i, int) else pl.multiple_of(i * rows, rows), rows)

    upbufs = (upbuf0, upbuf1)

    def ffn_up(j, buf):
        _slab_store(buf, 0, ucar[j])
        _slab_store(buf, hb, _ff_up_chunk(h2_ref[...], wup_ref, j, d_ff))
        ucar[j] = _slab_load(buf, tile, hb)

    def ffn_act(j, buf):
        w = _ff_cols(wcf_ref, j, d_ff)
        for r0 in range(0, tile, ffn_rows):
            cv = (_rows_times(_slab_window(buf, r0 + hb - 2, ffn_rows), w[0])
                  + _rows_times(_slab_window(buf, r0 + hb - 1, ffn_rows), w[1])
                  + _rows_times(_slab_load(buf, r0 + hb, ffn_rows), w[2]))
            a = _silu(cv[:, 0:FF_CHUNK]) * cv[:, FF_CHUNK:2 * FF_CHUNK]
            act_ref[j, r0:r0 + ffn_rows, :] = a.astype(_BF16)

    def ffn_down(j, n):
        acc = x2_ref[...]
        for i in range(n):
            r0 = (j + i) * FF_CHUNK
            if not isinstance(j, int):
                r0 = pl.multiple_of(r0, FF_CHUNK)
            acc = acc + jnp.dot(act_ref[j + i], wdown_ref[pl.ds(r0, FF_CHUNK), :],
                                preferred_element_type=_F32)
        x2_ref[...] = acc

    n_mix = tile // mix_rows
    n_iter = (n_chunks - 3) // 2
    assert n_chunks >= 3 and mix_first + n_iter * mix_per_pair <= n_mix

    def mixer_finish():
        x2_ref[...] = x_ref[...] + jnp.dot(mix_ref[...], wout_ref[...],
                                           preferred_element_type=_F32)
        norm_rows_static(x2_ref, g2_ref, h2_ref, _BF16)
        _slab_store(ubuf, 0, _slab_load(ubuf, tile, ha))
        _slab_store(zbuf, 0, _slab_load(zbuf, tile, hb))

    @pl.when(s == 0)
    def _():
        for i in range(n_mix):
            mixer_chunk(i)
        mixer_finish()

    @pl.when(has_ffn)
    def _():
        ffn_up(0, upbuf0)
        ffn_up(1, upbuf1)
        ffn_act(0, upbuf0)
        for i in range(mix_first):
            mixer_chunk(i)

        def ffn_pair(k, c):
            j = 2 * k
            ffn_up(j + 2, upbuf0)
            ffn_act(j + 1, upbuf1)
            ffn_up(j + 3, upbuf1)
            ffn_act(j + 2, upbuf0)
            ffn_down(j, 2)
            for i in range(mix_per_pair):
                mixer_chunk(mix_first + k * mix_per_pair + i)
            return c
        lax.fori_loop(0, n_iter, ffn_pair, 0)
        j_done = 2 * n_iter
        for j in range(j_done + 2, n_chunks):
            ffn_up(j, upbufs[j % 2])
            ffn_act(j - 1, upbufs[(j - 1) % 2])
        ffn_act(n_chunks - 1, upbufs[(n_chunks - 1) % 2])
        ffn_down(j_done, n_chunks - j_done)

        for i in range(mix_first + n_iter * mix_per_pair, n_mix):
            mixer_chunk(i)
        norm_rows_static(x2_ref, gf_ref, y_ref, _F32)
        mixer_finish()

    @pl.when(jnp.logical_and(has_ffn, t_ffn == nt - 1))
    def _():
        for j in range(n_chunks):
            last = ucar[j, hb - 2:hb, :]
            for half in range(2):
                c0 = half * d_ff + j * FF_CHUNK
                cf_ref[:, c0:c0 + FF_CHUNK] = last[:, half * FF_CHUNK:(half + 1) * FF_CHUNK]

    @pl.when(jnp.logical_and(has_mix, t_mix == nt - 1))
    def _():
        ca_ref[...] = _slab_window(ubuf, off_a, conv_a_w - 1)
        cb_ref[...] = _slab_window(zbuf, hb - 2, 2)


def _prompt_layer(x, meta, wts, *, tile, norm_rows, mix_rows, mix_first, mix_per_pair,
                  ffn_rows):
    nb, length, d = x.shape
    nt = length // tile
    n_tiles = nb * nt
    (g1, win, wa, ba, lng, lnb, wb, beta_a, beta_b, wout, g2, wup, wcf, wdown, gf) = wts
    a_width = wa.shape[-1]
    b_width = wb.shape[-1]
    conv_a_w = wa.shape[0]
    d_ff = wdown.shape[0]
    n_chunks = d_ff // FF_CHUNK
    ha = -(-(conv_a_w - 1) // SUBLANES) * SUBLANES
    hb = SUBLANES
    assert meta.shape[0] % (2 * SUBLANES) == 0 and hb <= meta.shape[0] <= tile
    kern = functools.partial(_seq_kernel, tile=tile, nt=nt, n_tiles=n_tiles,
                             norm_rows=norm_rows, mix_rows=mix_rows,
                             mix_first=mix_first, mix_per_pair=mix_per_pair, ffn_rows=ffn_rows,
                             a_width=a_width, b_width=b_width, conv_a_w=conv_a_w,
                             d_ff=d_ff)
    mix_tile = lambda s: jnp.minimum(s, n_tiles - 1)
    ffn_tile = lambda s: jnp.maximum(s - 1, 0)
    const_spec = lambda a: pl.BlockSpec(a.shape, lambda s: (0,) * a.ndim,
                                        pipeline_mode=pl.Buffered(1))
    consts = (meta,) + tuple(wts)
    in_specs = [pl.BlockSpec((None, tile, d),
                             lambda s: (mix_tile(s) // nt, mix_tile(s) % nt, 0))]
    in_specs += [const_spec(c) for c in consts]
    state_shapes = ((conv_a_w - 1, a_width), (2, b_width), (2, 2 * d_ff))
    state_tiles = (mix_tile, mix_tile, ffn_tile)
    out_shape = (jax.ShapeDtypeStruct((nb, length, d), _F32),) + tuple(
        jax.ShapeDtypeStruct((nb,) + shp, _F32) for shp in state_shapes)
    out_specs = (
        pl.BlockSpec((None, tile, d), lambda s: (ffn_tile(s) // nt, ffn_tile(s) % nt, 0)),
    ) + tuple(
        pl.BlockSpec((None,) + shp, lambda s, f=f, n=len(shp): (f(s) // nt,) + (0,) * n)
        for shp, f in zip(state_shapes, state_tiles))
    scratch = [
        pltpu.VMEM((a_width // LANES, ha + tile, LANES), _F32),
        pltpu.VMEM((b_width // LANES, hb + tile, LANES), _F32),
        pltpu.VMEM((n_chunks, hb, 2 * FF_CHUNK), _F32),
        pltpu.VMEM((a_width // LANES, ha, LANES), _F32),
        pltpu.VMEM((b_width // LANES, hb, LANES), _F32),
        pltpu.VMEM((n_chunks, hb, 2 * FF_CHUNK), _F32),
        pltpu.VMEM((2 * FF_CHUNK // LANES, hb + tile, LANES), _F32),
        pltpu.VMEM((2 * FF_CHUNK // LANES, hb + tile, LANES), _F32),
        pltpu.VMEM((tile, d), _BF16),
        pltpu.VMEM((tile, d), _BF16),
        pltpu.VMEM((tile, win.shape[1]), _F32),
        pltpu.VMEM((tile, a_width + b_width), _BF16),
        pltpu.VMEM((tile, d), _F32),
        pltpu.VMEM((n_chunks, tile, FF_CHUNK), _BF16),
    ]
    return pl.pallas_call(
        kern,
        grid=(n_tiles + 1,),
        in_specs=in_specs,
        out_specs=out_specs,
        out_shape=out_shape,
        scratch_shapes=scratch,
        compiler_params=pltpu.CompilerParams(
            dimension_semantics=("arbitrary",),
            vmem_limit_bytes=VMEM_LIMIT_BYTES),
        name="prompt_layer",
    )(x, *consts)


def _sample_kernel(xs_ref, sa_ref, sb_ref, sf_ref, g1_ref, win_ref, wa_ref, ba_ref,
                   lng_ref, lnb_ref, wb_ref, beta_a_ref, beta_b_ref, wout_ref, g2_ref,
                   wup_ref, wcf_ref, wdown_ref, gf_ref,
                   ys_ref, na_ref, nb_ref, nf_ref,
                   h_ref, proj_ref, u_ref, z_ref, mix_ref, x1_ref, up_ref, act_ref, xt_ref,
                   *, steps, d, a_width, b_width, conv_a_w, d_ff, n_chunks):
    s = xs_ref.shape[0]
    hist_a = conv_a_w - 1
    rows_of = lambda t: slice(t * s, (t + 1) * s)

    for t in range(steps):
        xt_ref[t] = xs_ref[:, t, :]
    for t in range(steps):
        h_ref[rows_of(t), :] = _rmsnorm(xt_ref[t], g1_ref[...]).astype(_BF16)
    proj_ref[...] = jnp.dot(h_ref[...], win_ref[...], preferred_element_type=_F32)

    o = 2 * a_width
    for t in range(steps):
        r = rows_of(t)
        u_ref[r, :] = proj_ref[r, 0:a_width] * _sigmoid(proj_ref[r, a_width:o])
        z_ref[r, :] = (proj_ref[r, o + b_width:o + 2 * b_width]
                       * proj_ref[r, o + 2 * b_width:o + 3 * b_width])

    def xe_a(j):
        return sa_ref[j] if j < hist_a else u_ref[rows_of(j - hist_a), :]

    def xe_b(j):
        return sb_ref[:, j, :] if j < 2 else z_ref[rows_of(j - 2), :]

    for t in range(steps):
        r = rows_of(t)
        acc = jnp.broadcast_to(ba_ref[...], (s, a_width))
        for k in range(conv_a_w):
            acc = acc + _rows_times(xe_a(t + k), wa_ref[k])
        ya = _group_a_out(acc, lng_ref[...], lnb_ref[...], beta_a_ref[...])
        mix_ref[r, 0:a_width] = ya.astype(_BF16)
        zb = (_rows_times(xe_b(t), wb_ref[0]) + _rows_times(xe_b(t + 1), wb_ref[1])
              + _rows_times(xe_b(t + 2), wb_ref[2]))
        yb = proj_ref[r, o:o + b_width] * zb * beta_b_ref[...]
        mix_ref[r, a_width:a_width + b_width] = yb.astype(_BF16)

    for j in range(hist_a):
        na_ref[j] = xe_a(j + steps)
    for j in range(2):
        nb_ref[:, j, :] = xe_b(j + steps)

    x1_ref[...] = jnp.dot(mix_ref[...], wout_ref[...], preferred_element_type=_F32)
    for t in range(steps):
        r = rows_of(t)
        x1 = x1_ref[r, :] + xt_ref[t]
        x1_ref[r, :] = x1
        h_ref[r, :] = _rmsnorm(x1, g2_ref[...]).astype(_BF16)

    for j in range(n_chunks):
        up_ref[...] = _ff_up_chunk(h_ref[...], wup_ref, j, d_ff)
        w = _ff_cols(wcf_ref, j, d_ff)
        lo = j * FF_CHUNK

        def xe_f(i, half):
            if i < 2:
                c0 = half * d_ff + lo
                return sf_ref[:, i, c0:c0 + FF_CHUNK]
            return up_ref[rows_of(i - 2), half * FF_CHUNK:(half + 1) * FF_CHUNK]

        for t in range(steps):
            gv = []
            for half in range(2):
                wh = w[:, :, half * FF_CHUNK:(half + 1) * FF_CHUNK]
                gv.append(_rows_times(xe_f(t, half), wh[0]) + _rows_times(xe_f(t + 1, half), wh[1])
                          + _rows_times(xe_f(t + 2, half), wh[2]))
            act_ref[j, rows_of(t), :] = (_silu(gv[0]) * gv[1]).astype(_BF16)
        for i in range(2):
            for half in range(2):
                c0 = half * d_ff + lo
                nf_ref[:, i, c0:c0 + FF_CHUNK] = xe_f(i + steps, half)

    acc = _ff_down_chain(x1_ref[...], act_ref, wdown_ref, n_chunks)
    for t in range(steps):
        ys_ref[:, t, :] = _rmsnorm(acc[rows_of(t), :], gf_ref[...])


def _sample_layer(xs, sa, sb, sf, wts, *, steps, seq_block):
    n = xs.shape[0]
    (g1, win, wa, ba, lng, lnb, wb, beta_a, beta_b, wout, g2, wup, wcf, wdown, gf) = wts
    d = g1.shape[1]
    a_width = wa.shape[-1]
    b_width = wb.shape[-1]
    conv_a_w = wa.shape[0]
    d_ff = wdown.shape[0]
    n_chunks = d_ff // FF_CHUNK
    m = steps * seq_block
    kern = functools.partial(_sample_kernel, steps=steps, d=d, a_width=a_width,
                             b_width=b_width, conv_a_w=conv_a_w, d_ff=d_ff,
                             n_chunks=n_chunks)
    seq_spec = lambda a: pl.BlockSpec((seq_block,) + a.shape[1:], lambda i: (i, 0, 0))
    row_spec = lambda a: (pl.BlockSpec((a.shape[0], seq_block, a.shape[2]), lambda i: (0, i, 0))
                          if a is sa else seq_spec(a))
    const_spec = lambda a: pl.BlockSpec(a.shape, lambda i: (0,) * a.ndim,
                                        pipeline_mode=pl.Buffered(1))
    data = (xs, sa, sb, sf)
    in_specs = [row_spec(a) for a in data] + [const_spec(a) for a in wts]
    out_shape = tuple(jax.ShapeDtypeStruct(a.shape, _F32) for a in data)
    out_specs = tuple(row_spec(a) for a in data)
    scratch = [
        pltpu.VMEM((m, d), _BF16),
        pltpu.VMEM((m, win.shape[1]), _F32),
        pltpu.VMEM((m, a_width), _F32),
        pltpu.VMEM((m, b_width), _F32),
        pltpu.VMEM((m, a_width + b_width), _BF16),
        pltpu.VMEM((m, d), _F32),
        pltpu.VMEM((m, 2 * FF_CHUNK), _F32),
        pltpu.VMEM((n_chunks, m, FF_CHUNK), _BF16),
        pltpu.VMEM((steps, seq_block, d), _F32),
    ]
    return pl.pallas_call(
        kern,
        grid=(n // seq_block,),
        in_specs=in_specs,
        out_specs=out_specs,
        out_shape=out_shape,
        scratch_shapes=scratch,
        compiler_params=pltpu.CompilerParams(
            dimension_semantics=("arbitrary",),
            vmem_limit_bytes=VMEM_LIMIT_BYTES),
        name="sample_layer",
    )(*data, *wts)


def kernel(x_prompt, x_sample, state_conv_a, state_conv_b, state_conv_ffn, meta_tokens,
           norm_mix_g, w_in, w_conv_a, b_conv_a, gn_a_g, gn_a_b, w_conv_b, beta_a, beta_b,
           w_out, norm_ffn_g, w_up, w_conv_f, w_down, norm_final_g):
    depth = w_in.shape[0]
    assert depth == 1, "single-layer step only"
    n_dec, dec_seq, _ = x_sample.shape
    a_width = w_conv_a.shape[2]
    d_ff = w_down.shape[1]
    assert x_prompt.shape[1] % PROMPT_TILING["tile"] == 0 and n_dec % SAMPLE_SEQ_BLOCK == 0
    assert a_width % LANES == 0 and LANES == 2 * HEAD_DIM
    assert d_ff % FF_CHUNK == 0 and w_conv_b.shape[1] == 3 and w_conv_f.shape[1] == 3

    row = lambda v: v.reshape(1, -1)
    tile8 = lambda w: jnp.broadcast_to(w[:, None, :], (w.shape[0], SUBLANES, w.shape[1]))
    wts = (
        row(norm_mix_g[0]),
        w_in[0].astype(_BF16),
        tile8(w_conv_a[0]), row(b_conv_a[0]), row(gn_a_g[0]), row(gn_a_b[0]),
        tile8(w_conv_b[0]), row(beta_a[0]), row(beta_b[0]),
        w_out[0].astype(_BF16),
        row(norm_ffn_g[0]),
        w_up[0].astype(_BF16),
        tile8(w_conv_f[0]),
        w_down[0].astype(_BF16),
        row(norm_final_g),
    )

    y_prompt, na_p, nb_p, nf_p = _prompt_layer(x_prompt, meta_tokens, wts, **PROMPT_TILING)

    y_s, na_s, nb_s, nf_s = _sample_layer(
        x_sample, jnp.transpose(state_conv_a[0], (1, 0, 2)), state_conv_b[0],
        state_conv_ffn[0], wts, steps=dec_seq, seq_block=SAMPLE_SEQ_BLOCK)
    na_s = jnp.transpose(na_s, (1, 0, 2))

    return (y_prompt, y_s, na_p[None], nb_p[None], nf_p[None], na_s[None], nb_s[None],
            nf_s[None])
```

```python
import functools

import jax
import jax.numpy as jnp
from jax import lax
from jax.experimental import pallas as pl
from jax.experimental.pallas import tpu as pltpu

EPS = 1e-6
HEAD_DIM = 64
LANES = 128
SUBLANES = 8
FF_CHUNK = 256
VMEM_LIMIT_BYTES = 56 * 1024 * 1024

PROMPT_TILING = dict(tile=512, norm_rows=64, mix_rows=32, mix_first=5, mix_per_pair=2,
                     ffn_rows=32)
SAMPLE_SEQ_BLOCK = 32

_F32 = jnp.float32
_BF16 = jnp.bfloat16


def _rmsnorm(x, g):
    ms = jnp.mean(x * x, axis=-1, keepdims=True)
    return x * lax.rsqrt(ms + EPS) * g


def _sigmoid(x):
    return 1.0 / (1.0 + jnp.exp(-x))


def _silu(x):
    return x * _sigmoid(x)


def _group_mean(v, lane_lo):
    sa = jnp.sum(jnp.where(lane_lo, v, 0.0), axis=1, keepdims=True)
    sb = jnp.sum(jnp.where(lane_lo, 0.0, v), axis=1, keepdims=True)
    return jnp.where(lane_lo, sa, sb) * (1.0 / HEAD_DIM)


def _head_layernorm(ua):
    rows, width = ua.shape
    lane_lo = lax.broadcasted_iota(jnp.int32, (rows, LANES), 1) < HEAD_DIM
    outs = []
    for j in range(width // LANES):
        xb = ua[:, j * LANES:(j + 1) * LANES]
        d = xb - _group_mean(xb, lane_lo)
        var = _group_mean(d * d, lane_lo)
        outs.append(d * lax.rsqrt(var + EPS))
    return jnp.concatenate(outs, axis=1)


def _group_a_out(ua, lng, lnb, beta_a):
    return _silu(_head_layernorm(ua) * lng + lnb) * beta_a


def _rows_times(x, w8):
    r, c = x.shape
    return (x.reshape(r // SUBLANES, SUBLANES, c) * w8).reshape(r, c)


def _ff_cols(ref, j, d_ff):
    off = j * FF_CHUNK
    if not isinstance(j, int):
        off = pl.multiple_of(off, FF_CHUNK)
    return jnp.concatenate([ref[:, :, pl.ds(off, FF_CHUNK)],
                            ref[:, :, pl.ds(d_ff + off, FF_CHUNK)]], axis=2)


def _ff_up_chunk(h, wup_ref, j, d_ff):
    off = j * FF_CHUNK
    if not isinstance(j, int):
        off = pl.multiple_of(off, FF_CHUNK)
    g = jnp.dot(h, wup_ref[:, pl.ds(off, FF_CHUNK)], preferred_element_type=_F32)
    v = jnp.dot(h, wup_ref[:, pl.ds(d_ff + off, FF_CHUNK)], preferred_element_type=_F32)
    return jnp.concatenate([g, v], axis=1)


def _ff_down_chain(acc, act_ref, wdown_ref, n_chunks):
    for j in range(n_chunks):
        acc = acc + jnp.dot(act_ref[j], wdown_ref[j * FF_CHUNK:(j + 1) * FF_CHUNK, :],
                            preferred_element_type=_F32)
    return acc


def _slab_store(buf, row0, val):
    for c in range(buf.shape[0]):
        buf[c, pl.ds(row0, val.shape[0]), :] = val[:, c * LANES:(c + 1) * LANES]


def _slab_load(buf, row0, nrows):
    return jnp.concatenate(
        [buf[c, pl.ds(row0, nrows), :] for c in range(buf.shape[0])], axis=1)


def _slab_window(buf, row0, nrows):
    return jnp.concatenate(
        [buf[c, pl.ds(row0, nrows, stride=1), :] for c in range(buf.shape[0])], axis=1)


def _seq_kernel(x_ref, meta_ref, g1_ref, win_ref, wa_ref, ba_ref,
                lng_ref, lnb_ref, wb_ref, beta_a_ref, beta_b_ref, wout_ref, g2_ref,
                wup_ref, wcf_ref, wdown_ref, gf_ref,
                y_ref, ca_ref, cb_ref, cf_ref,
                ubuf, zbuf, ucar, ca0_ref, cb0_ref, cf0_ref, upbuf0, upbuf1, h1_ref, h2_ref,
                proj_ref, mix_ref, x2_ref, act_ref,
                *, tile, nt, n_tiles, norm_rows, mix_rows, mix_first, mix_per_pair, ffn_rows,
                a_width, b_width, conv_a_w, d_ff):
    n_chunks = d_ff // FF_CHUNK
    s = pl.program_id(0)
    t_mix = lax.rem(s, nt)
    t_ffn = lax.rem(s + nt - 1, nt)
    has_mix = s < n_tiles
    has_ffn = s > 0
    ha = ubuf.shape[1] - tile
    hb = zbuf.shape[1] - tile
    off_a = ha - (conv_a_w - 1)

    def mixer_rows(r0, rows):
        p = lambda lo, w: proj_ref[pl.ds(r0, rows), lo:lo + w]
        u = p(0, a_width) * _sigmoid(p(a_width, a_width))
        _slab_store(ubuf, ha + r0, u)
        acc = jnp.broadcast_to(ba_ref[...], (rows, a_width))
        for k in range(conv_a_w):
            acc = acc + _rows_times(_slab_window(ubuf, r0 + off_a + k, rows), wa_ref[k])
        ya = _group_a_out(acc, lng_ref[...], lnb_ref[...], beta_a_ref[...])
        mix_ref[pl.ds(r0, rows), 0:a_width] = ya.astype(_BF16)

        o = 2 * a_width
        z = p(o + b_width, b_width) * p(o + 2 * b_width, b_width)
        _slab_store(zbuf, hb + r0, z)
        zb = (_rows_times(_slab_window(zbuf, r0 + hb - 2, rows), wb_ref[0])
              + _rows_times(_slab_window(zbuf, r0 + hb - 1, rows), wb_ref[1])
              + _rows_times(z, wb_ref[2]))
        yb = p(o, b_width) * zb * beta_b_ref[...]
        mix_ref[pl.ds(r0, rows), a_width:a_width + b_width] = yb.astype(_BF16)

    @pl.when(s == 0)
    def _():
        m = meta_ref.shape[0]
        _slab_store(ubuf, 0, jnp.zeros((ha, a_width), _F32))
        _slab_store(zbuf, 0, jnp.zeros((hb, b_width), _F32))
        h1_ref[0:m, :] = _rmsnorm(meta_ref[...], g1_ref[...]).astype(_BF16)
        proj_ref[0:m, :] = jnp.dot(h1_ref[0:m, :], win_ref[...], preferred_element_type=_F32)
        mixer_rows(0, m)
        xm = meta_ref[...] + jnp.dot(mix_ref[0:m, :], wout_ref[...],
                                     preferred_element_type=_F32)
        hm = _rmsnorm(xm, g2_ref[...]).astype(_BF16)
        for j in range(n_chunks):
            cf0_ref[j] = _ff_up_chunk(hm, wup_ref, j, d_ff)[m - hb:m, :]
        for c in range(ubuf.shape[0]):
            ca0_ref[c] = ubuf[c, m:m + ha, :]
        for c in range(zbuf.shape[0]):
            cb0_ref[c] = zbuf[c, m:m + hb, :]

    @pl.when(t_mix == 0)
    def _():
        for c in range(ubuf.shape[0]):
            ubuf[c, 0:ha, :] = ca0_ref[c]
        for c in range(zbuf.shape[0]):
            zbuf[c, 0:hb, :] = cb0_ref[c]

    @pl.when(jnp.logical_and(has_ffn, t_ffn == 0))
    def _():
        ucar[...] = cf0_ref[...]

    def norm_rows_static(src_ref, g_ref, dst_ref, dtype):
        for r0 in range(0, tile, norm_rows):
            x = src_ref[r0:r0 + norm_rows, :]
            dst_ref[r0:r0 + norm_rows, :] = _rmsnorm(x, g_ref[...]).astype(dtype)

    @pl.when(has_mix)
    def _():
        norm_rows_static(x_ref, g1_ref, h1_ref, _BF16)
        proj_ref[...] = jnp.dot(h1_ref[...], win_ref[...], preferred_element_type=_F32)

    def mixer_chunk(i):
        rows = mix_rows
        mixer_rows(i * rows if isinstance(i, int) else pl.multiple_of(i * rows, rows), rows)

    upbufs = (upbuf0, upbuf1)

    def ffn_up(j, buf):
        _slab_store(buf, 0, ucar[j])
        _slab_store(buf, hb, _ff_up_chunk(h2_ref[...], wup_ref, j, d_ff))
        ucar[j] = _slab_load(buf, tile, hb)

    def ffn_act(j, buf):
        w = _ff_cols(wcf_ref, j, d_ff)
        for r0 in range(0, tile, ffn_rows):
            cv = (_rows_times(_slab_window(buf, r0 + hb - 2, ffn_rows), w[0])
                  + _rows_times(_slab_window(buf, r0 + hb - 1, ffn_rows), w[1])
                  + _rows_times(_slab_load(buf, r0 + hb, ffn_rows), w[2]))
            a = _silu(cv[:, 0:FF_CHUNK]) * cv[:, FF_CHUNK:2 * FF_CHUNK]
            act_ref[j, r0:r0 + ffn_rows, :] = a.astype(_BF16)

    def ffn_down(j, n):
        acc = x2_ref[...]
        for i in range(n):
            r0 = (j + i) * FF_CHUNK
            if not isinstance(j, int):
                r0 = pl.multiple_of(r0, FF_CHUNK)
            acc = acc + jnp.dot(act_ref[j + i], wdown_ref[pl.ds(r0, FF_CHUNK), :],
                                preferred_element_type=_F32)
        x2_ref[...] = acc

    n_mix = tile // mix_rows
    n_iter = (n_chunks - 3) // 2
    assert n_chunks >= 3 and mix_first + n_iter * mix_per_pair <= n_mix

    def mixer_finish():
        x2_ref[...] = x_ref[...] + jnp.dot(mix_ref[...], wout_ref[...],
                                           preferred_element_type=_F32)
        norm_rows_static(x2_ref, g2_ref, h2_ref, _BF16)
        _slab_store(ubuf, 0, _slab_load(ubuf, tile, ha))
        _slab_store(zbuf, 0, _slab_load(zbuf, tile, hb))

    @pl.when(s == 0)
    def _():
        for i in range(n_mix):
            mixer_chunk(i)
        mixer_finish()

    @pl.when(has_ffn)
    def _():
        ffn_up(0, upbuf0)
        ffn_up(1, upbuf1)
        ffn_act(0, upbuf0)
        for i in range(mix_first):
            mixer_chunk(i)

        def ffn_pair(k, c):
            j = 2 * k
            ffn_up(j + 2, upbuf0)
            ffn_act(j + 1, upbuf1)
            ffn_up(j + 3, upbuf1)
            ffn_act(j + 2, upbuf0)
            ffn_down(j, 2)
            for i in range(mix_per_pair):
                mixer_chunk(mix_first + k * mix_per_pair + i)
            return c
        lax.fori_loop(0, n_iter, ffn_pair, 0)
        j_done = 2 * n_iter
        for j in range(j_done + 2, n_chunks):
            ffn_up(j, upbufs[j % 2])
            ffn_act(j - 1, upbufs[(j - 1) % 2])
        ffn_act(n_chunks - 1, upbufs[(n_chunks - 1) % 2])
        ffn_down(j_done, n_chunks - j_done)

        for i in range(mix_first + n_iter * mix_per_pair, n_mix):
            mixer_chunk(i)
        norm_rows_static(x2_ref, gf_ref, y_ref, _F32)
        mixer_finish()

    @pl.when(jnp.logical_and(has_ffn, t_ffn == nt - 1))
    def _():
        for j in range(n_chunks):
            last = ucar[j, hb - 2:hb, :]
            for half in range(2):
                c0 = half * d_ff + j * FF_CHUNK
                cf_ref[:, c0:c0 + FF_CHUNK] = last[:, half * FF_CHUNK:(half + 1) * FF_CHUNK]

    @pl.when(jnp.logical_and(has_mix, t_mix == nt - 1))
    def _():
        ca_ref[...] = _slab_window(ubuf, off_a, conv_a_w - 1)
        cb_ref[...] = _slab_window(zbuf, hb - 2, 2)


def _prompt_layer(x, meta, wts, *, tile, norm_rows, mix_rows, mix_first, mix_per_pair,
                  ffn_rows):
    nb, length, d = x.shape
    nt = length // tile
    n_tiles = nb * nt
    (g1, win, wa, ba, lng, lnb, wb, beta_a, beta_b, wout, g2, wup, wcf, wdown, gf) = wts
    a_width = wa.shape[-1]
    b_width = wb.shape[-1]
    conv_a_w = wa.shape[0]
    d_ff = wdown.shape[0]
    n_chunks = d_ff // FF_CHUNK
    ha = -(-(conv_a_w - 1) // SUBLANES) * SUBLANES
    hb = SUBLANES
    assert meta.shape[0] % (2 * SUBLANES) == 0 and hb <= meta.shape[0] <= tile
    kern = functools.partial(_seq_kernel, tile=tile, nt=nt, n_tiles=n_tiles,
                             norm_rows=norm_rows, mix_rows=mix_rows,
                             mix_first=mix_first, mix_per_pair=mix_per_pair, ffn_rows=ffn_rows,
                             a_width=a_width, b_width=b_width, conv_a_w=conv_a_w,
                             d_ff=d_ff)
    mix_tile = lambda s: jnp.minimum(s, n_tiles - 1)
    ffn_tile = lambda s: jnp.maximum(s - 1, 0)
    const_spec = lambda a: pl.BlockSpec(a.shape, lambda s: (0,) * a.ndim,
                                        pipeline_mode=pl.Buffered(1))
    consts = (meta,) + tuple(wts)
    in_specs = [pl.BlockSpec((None, tile, d),
                             lambda s: (mix_tile(s) // nt, mix_tile(s) % nt, 0))]
    in_specs += [const_spec(c) for c in consts]
    state_shapes = ((conv_a_w - 1, a_width), (2, b_width), (2, 2 * d_ff))
    state_tiles = (mix_tile, mix_tile, ffn_tile)
    out_shape = (jax.ShapeDtypeStruct((nb, length, d), _F32),) + tuple(
        jax.ShapeDtypeStruct((nb,) + shp, _F32) for shp in state_shapes)
    out_specs = (
        pl.BlockSpec((None, tile, d), lambda s: (ffn_tile(s) // nt, ffn_tile(s) % nt, 0)),
    ) + tuple(
        pl.BlockSpec((None,) + shp, lambda s, f=f, n=len(shp): (f(s) // nt,) + (0,) * n)
        for shp, f in zip(state_shapes, state_tiles))
    scratch = [
        pltpu.VMEM((a_width // LANES, ha + tile, LANES), _F32),
        pltpu.VMEM((b_width // LANES, hb + tile, LANES), _F32),
        pltpu.VMEM((n_chunks, hb, 2 * FF_CHUNK), _F32),
        pltpu.VMEM((a_width // LANES, ha, LANES), _F32),
        pltpu.VMEM((b_width // LANES, hb, LANES), _F32),
        pltpu.VMEM((n_chunks, hb, 2 * FF_CHUNK), _F32),
        pltpu.VMEM((2 * FF_CHUNK // LANES, hb + tile, LANES), _F32),
        pltpu.VMEM((2 * FF_CHUNK // LANES, hb + tile, LANES), _F32),
        pltpu.VMEM((tile, d), _BF16),
        pltpu.VMEM((tile, d), _BF16),
        pltpu.VMEM((tile, win.shape[1]), _F32),
        pltpu.VMEM((tile, a_width + b_width), _BF16),
        pltpu.VMEM((tile, d), _F32),
        pltpu.VMEM((n_chunks, tile, FF_CHUNK), _BF16),
    ]
    return pl.pallas_call(
        kern,
        grid=(n_tiles + 1,),
        in_specs=in_specs,
        out_specs=out_specs,
        out_shape=out_shape,
        scratch_shapes=scratch,
        compiler_params=pltpu.CompilerParams(
            dimension_semantics=("arbitrary",),
            vmem_limit_bytes=VMEM_LIMIT_BYTES),
        name="prompt_layer",
    )(x, *consts)


def _sample_kernel(xs_ref, sa_ref, sb_ref, sf_ref, g1_ref, win_ref, wa_ref, ba_ref,
                   lng_ref, lnb_ref, wb_ref, beta_a_ref, beta_b_ref, wout_ref, g2_ref,
                   wup_ref, wcf_ref, wdown_ref, gf_ref,
                   ys_ref, na_ref, nb_ref, nf_ref,
                   h_ref, proj_ref, u_ref, z_ref, mix_ref, x1_ref, up_ref, act_ref, xt_ref,
                   sft_ref, *, steps, d, a_width, b_width, conv_a_w, d_ff, n_chunks):
    s = xs_ref.shape[0]
    hist_a = conv_a_w - 1
    rows_of = lambda t: slice(t * s, (t + 1) * s)

    for t in range(steps):
        xt_ref[t] = xs_ref[:, t, :]
    for i in range(2):
        sft_ref[i] = sf_ref[:, i, :]
    for t in range(steps):
        h_ref[rows_of(t), :] = _rmsnorm(xt_ref[t], g1_ref[...]).astype(_BF16)
    proj_ref[...] = jnp.dot(h_ref[...], win_ref[...], preferred_element_type=_F32)

    o = 2 * a_width
    for t in range(steps):
        r = rows_of(t)
        u_ref[r, :] = proj_ref[r, 0:a_width] * _sigmoid(proj_ref[r, a_width:o])
        z_ref[r, :] = (proj_ref[r, o + b_width:o + 2 * b_width]
                       * proj_ref[r, o + 2 * b_width:o + 3 * b_width])

    def xe_a(j):
        return sa_ref[j] if j < hist_a else u_ref[rows_of(j - hist_a), :]

    def xe_b(j):
        return sb_ref[:, j, :] if j < 2 else z_ref[rows_of(j - 2), :]

    for t in range(steps):
        r = rows_of(t)
        acc = jnp.broadcast_to(ba_ref[...], (s, a_width))
        for k in range(conv_a_w):
            acc = acc + _rows_times(xe_a(t + k), wa_ref[k])
        ya = _group_a_out(acc, lng_ref[...], lnb_ref[...], beta_a_ref[...])
        mix_ref[r, 0:a_width] = ya.astype(_BF16)
        zb = (_rows_times(xe_b(t), wb_ref[0]) + _rows_times(xe_b(t + 1), wb_ref[1])
              + _rows_times(xe_b(t + 2), wb_ref[2]))
        yb = proj_ref[r, o:o + b_width] * zb * beta_b_ref[...]
        mix_ref[r, a_width:a_width + b_width] = yb.astype(_BF16)

    for j in range(hist_a):
        na_ref[j] = xe_a(j + steps)
    for j in range(2):
        nb_ref[:, j, :] = xe_b(j + steps)

    x1_ref[...] = jnp.dot(mix_ref[...], wout_ref[...], preferred_element_type=_F32)
    for t in range(steps):
        r = rows_of(t)
        x1 = x1_ref[r, :] + xt_ref[t]
        x1_ref[r, :] = x1
        h_ref[r, :] = _rmsnorm(x1, g2_ref[...]).astype(_BF16)

    for j in range(n_chunks):
        up_ref[...] = _ff_up_chunk(h_ref[...], wup_ref, j, d_ff)
        w = _ff_cols(wcf_ref, j, d_ff)
        lo = j * FF_CHUNK

        def xe_f(i, half):
            if i < 2:
                c0 = half * d_ff + lo
                return sft_ref[i, :, c0:c0 + FF_CHUNK]
            return up_ref[rows_of(i - 2), half * FF_CHUNK:(half + 1) * FF_CHUNK]

        for t in range(steps):
            gv = []
            for half in range(2):
                wh = w[:, :, half * FF_CHUNK:(half + 1) * FF_CHUNK]
                gv.append(_rows_times(xe_f(t, half), wh[0]) + _rows_times(xe_f(t + 1, half), wh[1])
                          + _rows_times(xe_f(t + 2, half), wh[2]))
            act_ref[j, rows_of(t), :] = (_silu(gv[0]) * gv[1]).astype(_BF16)
        for i in range(2):
            for half in range(2):
                c0 = half * d_ff + lo
                nf_ref[:, i, c0:c0 + FF_CHUNK] = xe_f(i + steps, half)

    acc = _ff_down_chain(x1_ref[...], act_ref, wdown_ref, n_chunks)
    for t in range(steps):
        ys_ref[:, t, :] = _rmsnorm(acc[rows_of(t), :], gf_ref[...])


def _sample_layer(xs, sa, sb, sf, wts, *, steps, seq_block):
    n = xs.shape[0]
    (g1, win, wa, ba, lng, lnb, wb, beta_a, beta_b, wout, g2, wup, wcf, wdown, gf) = wts
    d = g1.shape[1]
    a_width = wa.shape[-1]
    b_width = wb.shape[-1]
    conv_a_w = wa.shape[0]
    d_ff = wdown.shape[0]
    n_chunks = d_ff // FF_CHUNK
    m = steps * seq_block
    kern = functools.partial(_sample_kernel, steps=steps, d=d, a_width=a_width,
                             b_width=b_width, conv_a_w=conv_a_w, d_ff=d_ff,
                             n_chunks=n_chunks)
    seq_spec = lambda a: pl.BlockSpec((seq_block,) + a.shape[1:], lambda i: (i, 0, 0))
    row_spec = lambda a: (pl.BlockSpec((a.shape[0], seq_block, a.shape[2]), lambda i: (0, i, 0))
                          if a is sa else seq_spec(a))
    const_spec = lambda a: pl.BlockSpec(a.shape, lambda i: (0,) * a.ndim,
                                        pipeline_mode=pl.Buffered(1))
    data = (xs, sa, sb, sf)
    in_specs = [row_spec(a) for a in data] + [const_spec(a) for a in wts]
    out_shape = tuple(jax.ShapeDtypeStruct(a.shape, _F32) for a in data)
    out_specs = tuple(row_spec(a) for a in data)
    scratch = [
        pltpu.VMEM((m, d), _BF16),
        pltpu.VMEM((m, win.shape[1]), _F32),
        pltpu.VMEM((m, a_width), _F32),
        pltpu.VMEM((m, b_width), _F32),
        pltpu.VMEM((m, a_width + b_width), _BF16),
        pltpu.VMEM((m, d), _F32),
        pltpu.VMEM((m, 2 * FF_CHUNK), _F32),
        pltpu.VMEM((n_chunks, m, FF_CHUNK), _BF16),
        pltpu.VMEM((steps, seq_block, d), _F32),
        pltpu.VMEM((2, seq_block, 2 * d_ff), _F32),
    ]
    return pl.pallas_call(
        kern,
        grid=(n // seq_block,),
        in_specs=in_specs,
        out_specs=out_specs,
        out_shape=out_shape,
        scratch_shapes=scratch,
        compiler_params=pltpu.CompilerParams(
            dimension_semantics=("arbitrary",),
            vmem_limit_bytes=VMEM_LIMIT_BYTES),
        name="sample_layer",
    )(*data, *wts)


def kernel(x_prompt, x_sample, state_conv_a, state_conv_b, state_conv_ffn, meta_tokens,
           norm_mix_g, w_in, w_conv_a, b_conv_a, gn_a_g, gn_a_b, w_conv_b, beta_a, beta_b,
           w_out, norm_ffn_g, w_up, w_conv_f, w_down, norm_final_g):
    depth = w_in.shape[0]
    assert depth == 1, "single-layer step only"
    n_dec, dec_seq, _ = x_sample.shape
    a_width = w_conv_a.shape[2]
    d_ff = w_down.shape[1]
    assert x_prompt.shape[1] % PROMPT_TILING["tile"] == 0 and n_dec % SAMPLE_SEQ_BLOCK == 0
    assert a_width % LANES == 0 and LANES == 2 * HEAD_DIM
    assert d_ff % FF_CHUNK == 0 and w_conv_b.shape[1] == 3 and w_conv_f.shape[1] == 3

    row = lambda v: v.reshape(1, -1)
    tile8 = lambda w: jnp.broadcast_to(w[:, None, :], (w.shape[0], SUBLANES, w.shape[1]))
    wts = (
        row(norm_mix_g[0]),
        w_in[0].astype(_BF16),
        tile8(w_conv_a[0]), row(b_conv_a[0]), row(gn_a_g[0]), row(gn_a_b[0]),
        tile8(w_conv_b[0]), row(beta_a[0]), row(beta_b[0]),
        w_out[0].astype(_BF16),
        row(norm_ffn_g[0]),
        w_up[0].astype(_BF16),
        tile8(w_conv_f[0]),
        w_down[0].astype(_BF16),
        row(norm_final_g),
    )

    y_prompt, na_p, nb_p, nf_p = _prompt_layer(x_prompt, meta_tokens, wts, **PROMPT_TILING)

    y_s, na_s, nb_s, nf_s = _sample_layer(
        x_sample, jnp.transpose(state_conv_a[0], (1, 0, 2)), state_conv_b[0],
        state_conv_ffn[0], wts, steps=dec_seq, seq_block=SAMPLE_SEQ_BLOCK)
    na_s = jnp.transpose(na_s, (1, 0, 2))

    return (y_prompt, y_s, na_p[None], nb_p[None], nf_p[None], na_s[None], nb_s[None],
            nf_s[None])
```
